```python
import jax, jax.numpy as jnp
from jax import lax
import numpy as np

D_MODEL = 2048
BATCH = 8
SEQ = 2048
DEPTH = 1
DEC_BATCH = 128
DEC_SEQ = 4
PAST_LEN = 2048
PAGE_SIZE = 128

MIX_WIDTH = D_MODEL
SB_HEADS = 8
SB_HEAD_DIM = MIX_WIDTH // (2 * SB_HEADS)
SB_WIDTH = SB_HEADS * SB_HEAD_DIM
HG_HEADS = 8
HG_KDIM = (MIX_WIDTH - SB_WIDTH) // HG_HEADS
HG_VDIM = HG_KDIM
HG_KWIDTH = HG_HEADS * HG_KDIM
HG_VWIDTH = HG_HEADS * HG_VDIM
PROJ_SPLITS = (SB_WIDTH, 2 * SB_WIDTH, 3 * SB_WIDTH,
               3 * SB_WIDTH + HG_KWIDTH,
               3 * SB_WIDTH + 2 * HG_KWIDTH,
               3 * SB_WIDTH + 2 * HG_KWIDTH + HG_VWIDTH)
PROJ_WIDTH = 3 * SB_WIDTH + 2 * HG_KWIDTH + 2 * HG_VWIDTH
D_FF = 4 * D_MODEL
SB_BLOCK = 128
SB_BIAS_INIT = -7.0
HG_CHUNK = 32
RMS_EPS = 1e-6

kernel_name = "hymba_stickbreaking_hgrn2_decode_step"


def rms_norm(x, g):
    x32 = x.astype(jnp.float32)
    return x32 * lax.rsqrt(jnp.mean(x32 * x32, axis=-1, keepdims=True) + RMS_EPS) * g.astype(jnp.float32)


def stick_breaking(q, k, v, q_pos, k_pos, bias):
    z = jnp.einsum('bthd,bnhd->bhtn', q, k) * (SB_HEAD_DIM ** -0.5) \
        + bias.astype(jnp.float32)[None, :, None, None]
    causal = (k_pos[None, :] < q_pos[:, None])[None, None]
    log_beta = jax.nn.log_sigmoid(z)
    log_keep = jnp.where(causal, jax.nn.log_sigmoid(-z), 0.0)
    rc = lax.cumsum(log_keep, axis=3, reverse=True)
    after = jnp.concatenate([rc[..., 1:], jnp.zeros_like(rc[..., :1])], axis=3)
    weights = jnp.where(causal, jnp.exp(log_beta + after), 0.0)
    return jnp.einsum('bhtn,bnhd->bthd', weights, v)


def sb_prompt(q, k, v, bias):
    B, S, H, D = q.shape
    nb = S // SB_BLOCK
    k_pos = jnp.arange(S)
    q_blocks = q.reshape(B, nb, SB_BLOCK, H, D).transpose(1, 0, 2, 3, 4)

    def block(args):
        qb, i = args
        return stick_breaking(qb, k, v, i * SB_BLOCK + jnp.arange(SB_BLOCK), k_pos, bias)

    o = lax.map(block, (q_blocks, jnp.arange(nb)))
    return o.transpose(1, 0, 2, 3, 4).reshape(B, S, H, D)


def hgrn2_recurrence(q, k, g, v, s0):
    B, L, H, dk = q.shape
    dv = v.shape[-1]
    c = HG_CHUNK if L % HG_CHUNK == 0 else L
    n = L // c

    def chunks(a):
        return a.reshape(B, n, c, H, a.shape[-1]).transpose(1, 0, 3, 2, 4)

    incl = jnp.tril(jnp.ones((c, c), dtype=bool))[None, None, :, :, None]

    def step(S, blk):
        qb, kb, gb, vb = blk
        b = jnp.cumsum(gb, axis=2)
        b_last = b[:, :, -1, :]
        o_inter = jnp.einsum('bhtk,bhkv->bhtv', qb * jnp.exp(b), S)
        decay = jnp.exp(jnp.where(incl, b[:, :, :, None, :] - b[:, :, None, :, :], -jnp.inf))
        scores = jnp.einsum('bhtk,bhsk,bhtsk->bhts', qb, kb, decay)
        o = o_inter + jnp.einsum('bhts,bhsv->bhtv', scores, vb)
        S = S * jnp.exp(b_last)[..., None] + jnp.einsum(
            'bhsk,bhsv->bhkv', kb * jnp.exp(b_last[:, :, None, :] - b), vb)
        return S, o

    S, o = lax.scan(step, s0, (chunks(q), chunks(k), chunks(g), chunks(v)))
    o = o.transpose(1, 0, 3, 2, 4).reshape(B, L, H, dv)
    return o, S


def split_projection(xn, w_in):
    proj = jnp.einsum('bsd,de->bse', xn, w_in.astype(jnp.float32))
    return jnp.split(proj, PROJ_SPLITS, axis=-1)


def sb_heads(a):
    B, L, _ = a.shape
    return a.reshape(B, L, SB_HEADS, SB_HEAD_DIM)


def hgrn_gates(hq, hf, hi, lb):
    B, L, _ = hq.shape
    q = jax.nn.silu(hq).reshape(B, L, HG_HEADS, HG_KDIM)
    f = lb + (1.0 - lb) * jax.nn.sigmoid(hf.reshape(B, L, HG_HEADS, HG_KDIM))
    return q, 1.0 - f, jnp.log(f), hi.reshape(B, L, HG_HEADS, HG_VDIM)


def merge_heads(sb_o, hg_o, hg_gate, sb_norm_g, hg_norm_g, w_out):
    B, L = sb_o.shape[:2]
    sb = rms_norm(sb_o, sb_norm_g.reshape(SB_HEADS, SB_HEAD_DIM)).reshape(B, L, SB_WIDTH)
    hg = rms_norm(hg_o, hg_norm_g.reshape(HG_HEADS, HG_VDIM)).reshape(B, L, HG_VWIDTH) * jax.nn.silu(hg_gate)
    return jnp.einsum('bse,ed->bsd', jnp.concatenate([sb, hg], axis=-1), w_out.astype(jnp.float32))


def channel_mlp(h, g, w_up, w_down):
    u = jnp.einsum('bsd,df->bsf', rms_norm(h, g), w_up.astype(jnp.float32))
    return h + jnp.einsum('bsf,fd->bsd', jnp.square(jax.nn.relu(u)), w_down.astype(jnp.float32))


def setup_inputs(seed: int = 0) -> dict:
    key = jax.random.key(seed)
    ks = jax.random.split(key, 20)
    n_pages = PAST_LEN // PAGE_SIZE
    n_used = DEC_BATCH * n_pages
    n_phys = n_used + max(1, n_used // 4)
    page_table = jax.random.permutation(ks[0], n_phys)[:n_used].reshape(DEC_BATCH, n_pages).astype(jnp.int32)
    nrm = jax.random.normal
    return {
        'x_prompt': nrm(ks[1], (BATCH, SEQ, D_MODEL), jnp.float32),
        'x_sample': nrm(ks[2], (DEC_BATCH, DEC_SEQ, D_MODEL), jnp.float32),
        'cache_k': nrm(ks[3], (DEPTH, n_phys, PAGE_SIZE, SB_HEADS, SB_HEAD_DIM), jnp.float32),
        'cache_v': nrm(ks[4], (DEPTH, n_phys, PAGE_SIZE, SB_HEADS, SB_HEAD_DIM), jnp.float32),
        'state_hgrn': 0.5 * nrm(ks[5], (DEPTH, DEC_BATCH, HG_HEADS, HG_KDIM, HG_VDIM), jnp.float32),
        'page_table': page_table,
        'norm1_g': 1.0 + 0.05 * nrm(ks[6], (DEPTH, D_MODEL), jnp.float32),
        'w_in': nrm(ks[7], (DEPTH, D_MODEL, PROJ_WIDTH), jnp.float32) * D_MODEL ** -0.5,
        'sb_bias': SB_BIAS_INIT + 0.1 * nrm(ks[16], (DEPTH, SB_HEADS), jnp.float32),
        'sb_norm_g': 1.0 + 0.05 * nrm(ks[8], (DEPTH, SB_WIDTH), jnp.float32),
        'hg_norm_g': 1.0 + 0.05 * nrm(ks[9], (DEPTH, HG_VWIDTH), jnp.float32),
        'hg_lb_logits': 0.5 * nrm(ks[10], (DEPTH + 1, HG_KWIDTH), jnp.float32),
        'w_out': nrm(ks[11], (DEPTH, MIX_WIDTH, D_MODEL), jnp.float32) * MIX_WIDTH ** -0.5,
        'norm2_g': 1.0 + 0.05 * nrm(ks[12], (DEPTH, D_MODEL), jnp.float32),
        'w_up': nrm(ks[13], (DEPTH, D_MODEL, D_FF), jnp.float32) * D_MODEL ** -0.5,
        'w_down': nrm(ks[14], (DEPTH, D_FF, D_MODEL), jnp.float32) * D_FF ** -0.5,
        'final_norm_g': 1.0 + 0.05 * nrm(ks[15], (D_MODEL,), jnp.float32),
    }


def reference(x_prompt, x_sample, cache_k, cache_v, state_hgrn, page_table, norm1_g, w_in, sb_bias,
              sb_norm_g, hg_norm_g, hg_lb_logits, w_out, norm2_g, w_up, w_down, final_norm_g):
    f32 = jnp.float32
    n_seq, n_pages = page_table.shape
    past_len = n_pages * cache_k.shape[2]
    B, S = x_prompt.shape[:2]
    t_new = x_sample.shape[1]
    q_pos_s = past_len + jnp.arange(t_new)
    k_pos_s = jnp.arange(past_len + t_new)
    lb_all = jnp.cumsum(jax.nn.softmax(hg_lb_logits.astype(f32), axis=0), axis=0)
    hp = x_prompt.astype(f32)
    hs = x_sample.astype(f32)
    kp_rows, vp_rows, sp_rows, ks_rows, vs_rows, ss_rows = [], [], [], [], [], []
    for l in range(DEPTH):
        lb = lb_all[l].reshape(HG_HEADS, HG_KDIM)
        sq, sk, sv, hq, hf, hi, hgate = split_projection(rms_norm(hp, norm1_g[l]), w_in[l])
        sq, sk, sv = sb_heads(sq), sb_heads(sk), sb_heads(sv)
        sb_o = sb_prompt(sq, sk, sv, sb_bias[l])
        gq, gk, gg, gv = hgrn_gates(hq, hf, hi, lb)
        hg_o, s_p = hgrn2_recurrence(gq, gk, gg, gv, jnp.zeros((B, HG_HEADS, HG_KDIM, HG_VDIM), f32))
        hp = hp + merge_heads(sb_o, hg_o, hgate, sb_norm_g[l], hg_norm_g[l], w_out[l])
        hp = channel_mlp(hp, norm2_g[l], w_up[l], w_down[l])
        kp_rows.append(sk.astype(cache_k.dtype))
        vp_rows.append(sv.astype(cache_v.dtype))
        sp_rows.append(s_p.astype(state_hgrn.dtype))
        tq, tk, tv, uq, uf, ui, ugate = split_projection(rms_norm(hs, norm1_g[l]), w_in[l])
        tq, tk, tv = sb_heads(tq), sb_heads(tk), sb_heads(tv)
        past_k = cache_k[l][page_table].reshape(n_seq, past_len, SB_HEADS, SB_HEAD_DIM).astype(f32)
        past_v = cache_v[l][page_table].reshape(n_seq, past_len, SB_HEADS, SB_HEAD_DIM).astype(f32)
        keys = jnp.concatenate([past_k, tk], axis=1)
        vals = jnp.concatenate([past_v, tv], axis=1)
        sb_o2 = stick_breaking(tq, keys, vals, q_pos_s, k_pos_s, sb_bias[l])
        gq2, gk2, gg2, gv2 = hgrn_gates(uq, uf, ui, lb)
        hg_o2, s_s = hgrn2_recurrence(gq2, gk2, gg2, gv2, state_hgrn[l].astype(f32))
        hs = hs + merge_heads(sb_o2, hg_o2, ugate, sb_norm_g[l], hg_norm_g[l], w_out[l])
        hs = channel_mlp(hs, norm2_g[l], w_up[l], w_down[l])
        ks_rows.append(tk.astype(cache_k.dtype))
        vs_rows.append(tv.astype(cache_v.dtype))
        ss_rows.append(s_s.astype(state_hgrn.dtype))
    y_prompt = rms_norm(hp, final_norm_g).astype(x_prompt.dtype)
    y_sample = rms_norm(hs, final_norm_g).astype(x_sample.dtype)
    new_k_prompt = jnp.stack(kp_rows)
    new_v_prompt = jnp.stack(vp_rows)
    new_state_prompt = jnp.stack(sp_rows)
    new_k_sample = jnp.stack(ks_rows)
    new_v_sample = jnp.stack(vs_rows)
    new_state_sample = jnp.stack(ss_rows)
    return (y_prompt, y_sample, new_k_prompt, new_v_prompt, new_state_prompt, new_k_sample, new_v_sample, new_state_sample)
```

```python
import functools

import jax
import jax.numpy as jnp
from jax import lax
from jax.experimental import pallas as pl
from jax.experimental.pallas import tpu as pltpu

F32 = jnp.float32
BF16 = jnp.bfloat16

RMS_EPS = 1e-6
HEAD_DIM = 128
N_HEADS = 8
GROUP_WIDTH = N_HEADS * HEAD_DIM
HG_CHUNK = 32
HG_SUB = 8
VMEM_LIMIT = 56 * 1024 * 1024


def _params(*sem):
    return pltpu.CompilerParams(dimension_semantics=sem, vmem_limit_bytes=VMEM_LIMIT)


def _rms_normalize(x, g):
    return x * lax.rsqrt(jnp.mean(x * x, axis=-1, keepdims=True) + RMS_EPS) * g


def _split_hi_lo(x):
    hi = x.astype(BF16)
    lo = (x - hi.astype(F32)).astype(BF16)
    return hi, lo


def _dot_nt(a, b):
    return lax.dot_general(a, b, (((1,), (1,)), ((), ())), preferred_element_type=F32)


def _dot_tn(a, b):
    return lax.dot_general(a, b, (((0,), (0,)), ((), ())), preferred_element_type=F32)


def _sigmoid(x):
    return 1.0 / (1.0 + jnp.exp(-x))


def _silu(x):
    return x * _sigmoid(x)


def _inproj_kernel(x_ref, g_ref, w_ref, *refs):
    out_refs, xn_ref = refs[:-1], refs[-1]
    j = pl.program_id(1)

    @pl.when(j == 0)
    def _():
        xn_ref[...] = _rms_normalize(x_ref[...], g_ref[...]).astype(BF16)

    acc = jnp.dot(xn_ref[...], w_ref[...], preferred_element_type=F32)
    for n, o_ref in enumerate(out_refs):
        @pl.when(j == n)
        def _(o_ref=o_ref):
            o_ref[...] = acc.astype(o_ref.dtype)


_INPROJ_DTYPES = (BF16, F32, F32, BF16, F32, BF16, BF16)


def _inproj(x, g, w_bf16, tm):
    m, d = x.shape
    n_groups = w_bf16.shape[1] // GROUP_WIDTH
    assert n_groups == len(_INPROJ_DTYPES) and m % tm == 0
    return pl.pallas_call(
        _inproj_kernel,
        grid=(m // tm, n_groups),
        in_specs=[
            pl.BlockSpec((tm, d), lambda i, j: (i, 0)),
            pl.BlockSpec((1, d), lambda i, j: (0, 0)),
            pl.BlockSpec((d, GROUP_WIDTH), lambda i, j: (0, j)),
        ],
        out_specs=[pl.BlockSpec((tm, GROUP_WIDTH), lambda i, j: (i, 0)) for _ in _INPROJ_DTYPES],
        out_shape=[jax.ShapeDtypeStruct((m, GROUP_WIDTH), dt) for dt in _INPROJ_DTYPES],
        scratch_shapes=[pltpu.VMEM((tm, d), BF16)],
        compiler_params=_params("parallel", "arbitrary"),
        name="inproj",
    )(x, g, w_bf16)


def _suffix_matrix():
    r = lax.broadcasted_iota(jnp.int32, (2 * HEAD_DIM, 2 * HEAD_DIM), 0) % HEAD_DIM
    c = lax.broadcasted_iota(jnp.int32, (2 * HEAD_DIM, 2 * HEAD_DIM), 1)
    return jnp.where((c >= HEAD_DIM) | (r > c), 1.0, 0.0).astype(BF16)


def _log_break(z):
    l = jnp.log1p(jnp.exp(-jnp.abs(z)))
    log_beta = jnp.minimum(z, 0.0) - l
    return log_beta, log_beta - z


def _half_sums(log_keep, suffix_mat):
    hi, lo = _split_hi_lo(log_keep)
    out = []
    for s in range(log_keep.shape[1] // HEAD_DIM):
        sl = slice(s * HEAD_DIM, (s + 1) * HEAD_DIM)
        c = jnp.dot(jnp.concatenate([hi[:, sl], lo[:, sl]], axis=-1), suffix_mat,
                    preferred_element_type=F32)
        out.append((c[:, :HEAD_DIM], c[:, HEAD_DIM:]))
    return out


def _sb_block(z, carry, suffix_mat, causal):
    log_beta, log_keep = _log_break(z)
    if causal is not None:
        log_keep = jnp.where(causal, log_keep, 0.0)
    halves = _half_sums(log_keep, suffix_mat)
    after = []
    for suffix, total in reversed(halves):
        after.append(suffix + carry)
        carry = carry + total
    after = jnp.concatenate(after[::-1], axis=-1) if len(after) > 1 else after[0]
    w = jnp.exp(log_beta + after)
    if causal is not None:
        w = jnp.where(causal, w, 0.0)
    return w, carry


def _sb_prompt_kernel(bias_ref, q_ref, k_ref, v_ref, g_ref, o_ref, *, tq):
    h = pl.program_id(1)
    i = pl.program_id(2)
    scale = HEAD_DIM ** -0.5
    bias = bias_ref[h]
    q = q_ref[...]
    suffix_mat = _suffix_matrix()

    def block(j, acc, carry, causal):
        start = pl.multiple_of(j * tq, tq)
        kb = k_ref[pl.ds(start, tq), :].astype(BF16)
        vb = v_ref[pl.ds(start, tq), :].astype(BF16)
        z = _dot_nt(q, kb) * scale + bias
        w, carry = _sb_block(z, carry, suffix_mat, causal)
        return acc + jnp.dot(w.astype(BF16), vb, preferred_element_type=F32), carry

    row = lax.broadcasted_iota(jnp.int32, (tq, tq), 0)
    col = lax.broadcasted_iota(jnp.int32, (tq, tq), 1)
    zeros = jnp.zeros((tq, HEAD_DIM), F32)
    acc, carry = block(i, zeros, zeros, col < row)

    def body(jj, state):
        return block(i - 1 - jj, state[0], state[1], None)

    acc, carry = lax.fori_loop(0, i, body, (acc, carry))
    o_ref[...] = _rms_normalize(acc, g_ref[...]).astype(o_ref.dtype)


def _sb_prompt(q, k, v, bias, g, batch, seq, tq):
    nq = seq // tq
    return pl.pallas_call(
        functools.partial(_sb_prompt_kernel, tq=tq),
        grid=(batch, N_HEADS, nq),
        in_specs=[
            pl.BlockSpec(memory_space=pltpu.SMEM),
            pl.BlockSpec((tq, HEAD_DIM), lambda b, h, i: (b * nq + i, h)),
            pl.BlockSpec((seq, HEAD_DIM), lambda b, h, i: (b, h)),
            pl.BlockSpec((seq, HEAD_DIM), lambda b, h, i: (b, h)),
            pl.BlockSpec((1, HEAD_DIM), lambda b, h, i: (0, h)),
        ],
        out_specs=pl.BlockSpec((tq, HEAD_DIM), lambda b, h, i: (b * nq + i, h)),
        out_shape=jax.ShapeDtypeStruct((batch * seq, GROUP_WIDTH), BF16),
        compiler_params=_params("parallel", "parallel", "arbitrary"),
        name="sb_prompt",
    )(bias, q, k, v, g)


def _sb_decode_kernel(pt_ref, bias_ref, q_ref, kn_ref, vn_ref, ck_ref, cv_ref, g_ref, o_ref,
                      qbd_ref, acc_ref, carry_ref, *, t_new):
    del pt_ref
    p = pl.program_id(1)
    rows = N_HEADS * t_new
    scale = HEAD_DIM ** -0.5
    suffix_mat = _suffix_matrix()
    row_head = lax.broadcasted_iota(jnp.int32, (rows, HEAD_DIM), 0) // t_new
    bias = jnp.zeros((rows, HEAD_DIM), F32)
    for h in range(N_HEADS):
        bias = jnp.where(row_head == h, bias_ref[h], bias)

    def attend(z_acc, carry, causal, values):
        w, carry = _sb_block(z_acc * scale + bias, carry, suffix_mat, causal)
        o = jnp.zeros((rows, HEAD_DIM), F32)
        for h in range(N_HEADS):
            wh = jnp.where(row_head == h, w, 0.0).astype(BF16)
            o = o + jnp.dot(wh, values(h), preferred_element_type=F32)
        return o, carry

    @pl.when(p == 0)
    def _():
        q = q_ref[0].astype(F32)
        qrep = jnp.concatenate([q] * N_HEADS, axis=0).reshape(N_HEADS, t_new, GROUP_WIDTH)
        hh = lax.broadcasted_iota(jnp.int32, qrep.shape, 0)
        cc = lax.broadcasted_iota(jnp.int32, qrep.shape, 2) // HEAD_DIM
        qbd = jnp.where(hh == cc, qrep, 0).reshape(rows, GROUP_WIDTH)
        qbd = qbd.astype(BF16)
        qbd_ref[...] = qbd
        kn = jnp.concatenate([kn_ref[0], jnp.zeros((HEAD_DIM - t_new, GROUP_WIDTH), F32)], axis=0)
        vn = jnp.concatenate([vn_ref[0], jnp.zeros((HEAD_DIM - t_new, GROUP_WIDTH), F32)], axis=0)
        kn = kn.astype(BF16)
        vn = vn.astype(BF16)
        z = jnp.zeros((rows, HEAD_DIM), F32)
        for h in range(N_HEADS):
            sl = slice(h * HEAD_DIM, (h + 1) * HEAD_DIM)
            z = z + _dot_nt(qbd[:, sl], kn[:, sl])
        t_idx = lax.broadcasted_iota(jnp.int32, (rows, HEAD_DIM), 0) % t_new
        j_idx = lax.broadcasted_iota(jnp.int32, (rows, HEAD_DIM), 1)
        o, carry = attend(z, jnp.zeros((rows, HEAD_DIM), F32), j_idx < t_idx,
                          lambda h: vn[:, h * HEAD_DIM:(h + 1) * HEAD_DIM])
        acc_ref[...] = o
        carry_ref[...] = carry

    z = jnp.zeros((rows, HEAD_DIM), F32)
    for h in range(N_HEADS):
        kh = ck_ref[0, 0, :, h, :].astype(BF16)
        z = z + _dot_nt(qbd_ref[:, h * HEAD_DIM:(h + 1) * HEAD_DIM], kh)
    o, carry = attend(z, carry_ref[...], None, lambda h: cv_ref[0, 0, :, h, :].astype(BF16))
    acc_ref[...] += o
    carry_ref[...] = carry

    @pl.when(p == pl.num_programs(1) - 1)
    def _():
        o_ref[0] = _rms_normalize(acc_ref[...], g_ref[...])


def _sb_decode(q, k_new, v_new, cache_k, cache_v, page_table, bias, g_rows, layer):
    n_seq, t_new, _ = q.shape
    n_pages = page_table.shape[1]
    page = cache_k.shape[2]
    assert page == HEAD_DIM and cache_k.shape[3:] == (N_HEADS, HEAD_DIM)
    rows = N_HEADS * t_new
    row_spec = pl.BlockSpec((1, t_new, GROUP_WIDTH), lambda s, p, pt: (s, 0, 0))

    def page_map(s, p, pt):
        return (layer, pt[s * n_pages + n_pages - 1 - p], 0, 0, 0)

    page_spec = pl.BlockSpec((1, 1, page, N_HEADS, HEAD_DIM), page_map)
    return pl.pallas_call(
        functools.partial(_sb_decode_kernel, t_new=t_new),
        grid_spec=pltpu.PrefetchScalarGridSpec(
            num_scalar_prefetch=1,
            grid=(n_seq, n_pages),
            in_specs=[
                pl.BlockSpec(memory_space=pltpu.SMEM),
                row_spec, row_spec, row_spec, page_spec, page_spec,
                pl.BlockSpec((rows, HEAD_DIM), lambda s, p, pt: (0, 0)),
            ],
            out_specs=pl.BlockSpec((1, rows, HEAD_DIM), lambda s, p, pt: (s, 0, 0)),
            scratch_shapes=[
                pltpu.VMEM((rows, GROUP_WIDTH), BF16),
                pltpu.VMEM((rows, HEAD_DIM), F32),
                pltpu.VMEM((rows, HEAD_DIM), F32),
            ],
        ),
        out_shape=jax.ShapeDtypeStruct((n_seq, rows, HEAD_DIM), F32),
        compiler_params=_params("parallel", "arbitrary"),
        name="sb_decode",
    )(page_table.reshape(-1), bias, q, k_new, v_new, cache_k, cache_v, g_rows)


def _lower_bound(logits, layer):
    e = jnp.exp(logits - jnp.max(logits, axis=0, keepdims=True))
    return jnp.sum(e[:layer + 1], axis=0, keepdims=True) / jnp.sum(e, axis=0, keepdims=True)


def _hgrn_gates(hq, hf, lb):
    f = lb + (1.0 - lb) * _sigmoid(hf)
    return _silu(hq), 1.0 - f, jnp.log(f)


def _block_tril(n, blk):
    r = lax.broadcasted_iota(jnp.int32, (n, n), 0)
    c = lax.broadcasted_iota(jnp.int32, (n, n), 1)
    return jnp.where((c <= r) & (r // blk == c // blk), 1.0, 0.0).astype(BF16)


def _hgrn_prompt_kernel(hq_ref, hf_ref, hi_ref, hg_ref, lbl_ref, g_ref, o_ref, s_ref, st_ref,
                        *, tc, layer):
    t = pl.program_id(2)
    n_chunks = tc // HG_CHUNK
    n_sub = HG_CHUNK // HG_SUB

    @pl.when(t == 0)
    def _():
        st_ref[...] = jnp.zeros_like(st_ref)

    lb = _lower_bound(lbl_ref[...], layer)
    q, kk, g = _hgrn_gates(hq_ref[...].astype(F32), hf_ref[...], lb)
    v = hi_ref[...].astype(F32)
    v_bf = v.astype(BF16)

    g_hi, g_lo = _split_hi_lo(g)
    tril_chunk = _block_tril(tc, HG_CHUNK)
    tril_sub = _block_tril(tc, HG_SUB)
    b = (jnp.dot(tril_chunk, g_hi, preferred_element_type=F32)
         + jnp.dot(tril_chunk, g_lo, preferred_element_type=F32))
    c = (jnp.dot(tril_sub, g_hi, preferred_element_type=F32)
         + jnp.dot(tril_sub, g_lo, preferred_element_type=F32))

    sub_shape = (tc // HG_SUB, HG_SUB, HEAD_DIM)
    r_sub = lax.broadcasted_iota(jnp.int32, sub_shape, 1)
    q3, k3, c3, v3 = (a.reshape(sub_shape) for a in (q, kk, c, v))
    prods, shifted_v = [(q3 * k3)], [v3]
    for delta in range(1, HG_SUB):
        ks = pltpu.roll(k3, delta, axis=1)
        cs = pltpu.roll(c3, delta, axis=1)
        decay = jnp.exp(jnp.minimum(c3 - cs, 0.0))
        prods.append(jnp.where(r_sub >= delta, q3 * ks * decay, 0.0))
        shifted_v.append(pltpu.roll(v3, delta, axis=1))
    ones = jnp.ones((HEAD_DIM, HEAD_DIM), BF16)
    prods = jnp.concatenate(prods, axis=0).reshape(HG_SUB * tc, HEAD_DIM).astype(BF16)
    scores = jnp.dot(prods, ones, preferred_element_type=F32)
    o = jnp.zeros((tc, HEAD_DIM), F32)
    for delta in range(HG_SUB):
        o = o + scores[delta * tc:(delta + 1) * tc] * shifted_v[delta].reshape(tc, HEAD_DIM)

    chunk_shape = (n_chunks, HG_CHUNK, HEAD_DIM)
    r_chunk = lax.broadcasted_iota(jnp.int32, chunk_shape, 1)
    b3 = b.reshape(chunk_shape)
    kk3 = kk.reshape(chunk_shape)
    q_sub = (q * jnp.exp(c)).reshape(chunk_shape)
    q_cat, k_cat = [], []
    for i in range(1, n_sub):
        lo_row = i * HG_SUB
        b_start = b3[:, lo_row - 1:lo_row, :]
        k_cat.append(jnp.where(r_chunk < lo_row,
                               kk3 * jnp.exp(jnp.minimum(b_start - b3, 0.0)), 0.0).astype(BF16))
        q_cat.append(jnp.where((r_chunk >= lo_row) & (r_chunk < lo_row + HG_SUB),
                               q_sub, 0.0).astype(BF16))
    q_cat = jnp.concatenate(q_cat, axis=-1)
    k_cat = jnp.concatenate(k_cat, axis=-1)
    a_off = jnp.einsum('ctk,csk->cts', q_cat, k_cat, preferred_element_type=F32)
    o_off = jnp.einsum('cts,csv->ctv', a_off.astype(BF16), v_bf.reshape(chunk_shape),
                       preferred_element_type=F32)
    o = o + o_off.reshape(tc, HEAD_DIM)

    q_in = (q * jnp.exp(b)).astype(BF16)
    st = st_ref[...]
    o_inter = []
    for ci in range(n_chunks):
        rows = slice(ci * HG_CHUNK, (ci + 1) * HG_CHUNK)
        b_c = b[rows]
        b_last = b_c[HG_CHUNK - 1:HG_CHUNK]
        o_inter.append(_dot_nt(q_in[rows], st.astype(BF16)))
        k_out = (kk[rows] * jnp.exp(b_last - b_c)).astype(BF16)
        st = st * jnp.exp(b_last) + _dot_tn(v_bf[rows], k_out)
    st_ref[...] = st
    o = o + jnp.concatenate(o_inter, axis=0)

    o_ref[...] = (_rms_normalize(o, g_ref[...]) * _silu(hg_ref[...].astype(F32))).astype(o_ref.dtype)

    @pl.when(t == pl.num_programs(2) - 1)
    def _():
        s_ref[0, 0] = st.T


def _hgrn_prompt(hq, hf, hi, hg, lb_logits, g, batch, seq, tc, layer):
    nt = seq // tc
    n_layers = lb_logits.shape[0]
    tok = lambda b, h, t: (b * nt + t, h)
    return pl.pallas_call(
        functools.partial(_hgrn_prompt_kernel, tc=tc, layer=layer),
        grid=(batch, N_HEADS, nt),
        in_specs=[
            pl.BlockSpec((tc, HEAD_DIM), tok),
            pl.BlockSpec((tc, HEAD_DIM), tok),
            pl.BlockSpec((tc, HEAD_DIM), tok),
            pl.BlockSpec((tc, HEAD_DIM), tok),
            pl.BlockSpec((n_layers, HEAD_DIM), lambda b, h, t: (0, h)),
            pl.BlockSpec((1, HEAD_DIM), lambda b, h, t: (0, h)),
        ],
        out_specs=[
            pl.BlockSpec((tc, HEAD_DIM), tok),
            pl.BlockSpec((1, 1, HEAD_DIM, HEAD_DIM), lambda b, h, t: (b, h, 0, 0)),
        ],
        out_shape=[
            jax.ShapeDtypeStruct((batch * seq, GROUP_WIDTH), BF16),
            jax.ShapeDtypeStruct((batch, N_HEADS, HEAD_DIM, HEAD_DIM), F32),
        ],
        scratch_shapes=[pltpu.VMEM((HEAD_DIM, HEAD_DIM), F32)],
        compiler_params=_params("parallel", "parallel", "arbitrary"),
        name="hgrn_prompt",
    )(hq, hf, hi, hg, lb_logits, g)


def _hgrn_decode_kernel(hq_ref, hf_ref, hi_ref, hg_ref, s_ref, lbl_ref, g_ref, o_ref, so_ref,
                        *, t_new, layer):
    for h in range(N_HEADS):
        sl = slice(h * HEAD_DIM, (h + 1) * HEAD_DIM)
        lb = _lower_bound(lbl_ref[:, sl], layer)
        q, kk, g = _hgrn_gates(hq_ref[0, :, sl].astype(F32), hf_ref[0, :, sl], lb)
        v = hi_ref[0, :, sl].astype(F32)
        s = s_ref[0, 0, h]

        b_rows = [g[0:1]]
        for t in range(1, t_new):
            b_rows.append(b_rows[-1] + g[t:t + 1])
        b_last = b_rows[-1]

        o_rows = []
        for t in range(t_new):
            o_t = jnp.dot((q[t:t + 1] * jnp.exp(b_rows[t])).astype(BF16), s.astype(BF16),
                          preferred_element_type=F32)
            for u in range(t + 1):
                a = jnp.sum(q[t:t + 1] * kk[u:u + 1] * jnp.exp(b_rows[t] - b_rows[u]),
                            axis=-1, keepdims=True)
                o_t = o_t + a * v[u:u + 1]
            o_rows.append(o_t)
        o = jnp.concatenate(o_rows, axis=0)
        o_ref[0, :, sl] = _rms_normalize(o, g_ref[:, sl]) * _silu(hg_ref[0, :, sl].astype(F32))

        cols = [kk[u:u + 1] * jnp.exp(b_last - b_rows[u]) for u in range(t_new)] + [jnp.exp(b_last)]
        cols += [jnp.zeros_like(b_last)] * (HG_SUB - len(cols))
        cols_t = jnp.concatenate(cols, axis=0).T
        s_new = s * cols_t[:, t_new:t_new + 1]
        for u in range(t_new):
            s_new = s_new + cols_t[:, u:u + 1] * v[u:u + 1]
        so_ref[0, 0, h] = s_new


def _hgrn_decode(hq, hf, hi, hg, state, lb_logits, g, layer):
    n_seq, t_new, _ = hq.shape
    assert t_new < HG_SUB
    n_layers = lb_logits.shape[0]
    row_spec = pl.BlockSpec((1, t_new, GROUP_WIDTH), lambda s: (s, 0, 0))
    state_spec = pl.BlockSpec((1, 1, N_HEADS, HEAD_DIM, HEAD_DIM), lambda s: (layer, s, 0, 0, 0))
    return pl.pallas_call(
        functools.partial(_hgrn_decode_kernel, t_new=t_new, layer=layer),
        grid=(n_seq,),
        in_specs=[
            row_spec, row_spec, row_spec, row_spec, state_spec,
            pl.BlockSpec((n_layers, GROUP_WIDTH), lambda s: (0, 0)),
            pl.BlockSpec((1, GROUP_WIDTH), lambda s: (0, 0)),
        ],
        out_specs=[
            row_spec,
            pl.BlockSpec((1, 1, N_HEADS, HEAD_DIM, HEAD_DIM), lambda s: (0, s, 0, 0, 0)),
        ],
        out_shape=[
            jax.ShapeDtypeStruct((n_seq, t_new, GROUP_WIDTH), F32),
            jax.ShapeDtypeStruct((1, n_seq, N_HEADS, HEAD_DIM, HEAD_DIM), F32),
        ],
        compiler_params=_params("parallel"),
        name="hgrn_decode",
    )(hq, hf, hi, hg, state, lb_logits, g)


def _outproj_kernel(sb_ref, hg_ref, w_ref, x_ref, o_ref):
    half = sb_ref.shape[1]
    acc = jnp.dot(sb_ref[...].astype(BF16), w_ref[:half, :], preferred_element_type=F32)
    acc = acc + jnp.dot(hg_ref[...].astype(BF16), w_ref[half:, :], preferred_element_type=F32)
    o_ref[...] = x_ref[...] + acc


def _outproj(sb, hg, w_bf16, x, tm, tn):
    m, d = x.shape
    width = sb.shape[1]
    return pl.pallas_call(
        _outproj_kernel,
        grid=(m // tm, d // tn),
        in_specs=[
            pl.BlockSpec((tm, width), lambda i, j: (i, 0)),
            pl.BlockSpec((tm, width), lambda i, j: (i, 0)),
            pl.BlockSpec((2 * width, tn), lambda i, j: (0, j)),
            pl.BlockSpec((tm, tn), lambda i, j: (i, j)),
        ],
        out_specs=pl.BlockSpec((tm, tn), lambda i, j: (i, j)),
        out_shape=jax.ShapeDtypeStruct((m, d), F32),
        compiler_params=_params("parallel", "arbitrary"),
        name="outproj",
    )(sb, hg, w_bf16, x)


def _mlp_kernel(h_ref, g2_ref, wu_ref, wd_ref, gf_ref, o_ref, hn_ref, acc_ref, *, final_norm):
    j = pl.program_id(1)

    @pl.when(j == 0)
    def _():
        hn_ref[...] = _rms_normalize(h_ref[...], g2_ref[...]).astype(BF16)
        acc_ref[...] = jnp.zeros_like(acc_ref)

    u = jnp.dot(hn_ref[...], wu_ref[...], preferred_element_type=F32)
    a = jnp.square(jnp.maximum(u, 0.0)).astype(BF16)
    acc_ref[...] += jnp.dot(a, wd_ref[...], preferred_element_type=F32)

    @pl.when(j == pl.num_programs(1) - 1)
    def _():
        y = h_ref[...] + acc_ref[...]
        o_ref[...] = _rms_normalize(y, gf_ref[...]) if final_norm else y


def _mlp(h, g2, wu_bf16, wd_bf16, gf, tm, tf, final_norm):
    m, d = h.shape
    f = wu_bf16.shape[1]
    return pl.pallas_call(
        functools.partial(_mlp_kernel, final_norm=final_norm),
        grid=(m // tm, f // tf),
        in_specs=[
            pl.BlockSpec((tm, d), lambda i, j: (i, 0)),
            pl.BlockSpec((1, d), lambda i, j: (0, 0)),
            pl.BlockSpec((d, tf), lambda i, j: (0, j)),
            pl.BlockSpec((tf, d), lambda i, j: (j, 0)),
            pl.BlockSpec((1, d), lambda i, j: (0, 0)),
        ],
        out_specs=pl.BlockSpec((tm, d), lambda i, j: (i, 0)),
        out_shape=jax.ShapeDtypeStruct((m, d), F32),
        scratch_shapes=[pltpu.VMEM((tm, d), BF16), pltpu.VMEM((tm, d), F32)],
        compiler_params=_params("parallel", "arbitrary"),
        name="mlp",
    )(h, g2, wu_bf16, wd_bf16, gf)


def _row_tile(m, target):
    return target if m % target == 0 else m


def kernel(x_prompt, x_sample, cache_k, cache_v, state_hgrn, page_table, norm1_g, w_in, sb_bias,
           sb_norm_g, hg_norm_g, hg_lb_logits, w_out, norm2_g, w_up, w_down, final_norm_g):
    batch, seq, d = x_prompt.shape
    n_seq, t_new, _ = x_sample.shape
    depth = w_in.shape[0]
    hp = x_prompt.reshape(batch * seq, d)
    hs = x_sample.reshape(n_seq * t_new, d)
    tm_p = _row_tile(batch * seq, 512)
    tm_s = _row_tile(n_seq * t_new, 512)
    tq = _row_tile(seq, 256)
    gf = final_norm_g.reshape(1, d)

    outs = {k: [] for k in ("kp", "vp", "sp", "ks", "vs", "ss")}
    for l in range(depth):
        g1 = norm1_g[l].reshape(1, d)
        g2 = norm2_g[l].reshape(1, d)
        sbg = sb_norm_g[l].reshape(1, GROUP_WIDTH)
        hgg = hg_norm_g[l].reshape(1, GROUP_WIDTH)
        w_in_l = w_in[l].astype(BF16)
        w_out_l = w_out[l].astype(BF16)
        w_up_l = w_up[l].astype(BF16)
        w_down_l = w_down[l].astype(BF16)
        last = l == depth - 1

        sq, sk, sv, hq, hf, hi, hg = _inproj(hp, g1, w_in_l, tm_p)
        sb = _sb_prompt(sq, sk, sv, sb_bias[l], sbg, batch, seq, tq)
        hgo, s_p = _hgrn_prompt(hq, hf, hi, hg, hg_lb_logits, hgg, batch, seq, tq, l)
        hmid = _outproj(sb, hgo, w_out_l, hp, tm_p, 1024)
        hp = _mlp(hmid, g2, w_up_l, w_down_l, gf, tm_p, 1024, last)
        outs["kp"].append(sk.reshape(batch, seq, N_HEADS, HEAD_DIM))
        outs["vp"].append(sv.reshape(batch, seq, N_HEADS, HEAD_DIM))
        outs["sp"].append(s_p)

        tq_, tk, tv, uq, uf, ui, ug = (a.reshape(n_seq, t_new, GROUP_WIDTH)
                                       for a in _inproj(hs, g1, w_in_l, tm_s))
        sbg_rows = jnp.repeat(sbg.reshape(N_HEADS, HEAD_DIM), t_new, axis=0)
        sb2 = _sb_decode(tq_, tk, tv, cache_k, cache_v, page_table, sb_bias[l], sbg_rows, l)
        sb2 = sb2.reshape(n_seq, N_HEADS, t_new, HEAD_DIM).transpose(0, 2, 1, 3)
        sb2 = sb2.reshape(n_seq * t_new, GROUP_WIDTH)
        hgo2, s_s = _hgrn_decode(uq, uf, ui, ug, state_hgrn, hg_lb_logits, hgg, l)
        hmid2 = _outproj(sb2, hgo2.reshape(n_seq * t_new, GROUP_WIDTH), w_out_l, hs, tm_s, 1024)
        hs = _mlp(hmid2, g2, w_up_l, w_down_l, gf, tm_s, 1024, last)
        outs["ks"].append(tk.reshape(n_seq, t_new, N_HEADS, HEAD_DIM))
        outs["vs"].append(tv.reshape(n_seq, t_new, N_HEADS, HEAD_DIM))
        outs["ss"].append(s_s[0])

    y_prompt = hp.reshape(batch, seq, d)
    y_sample = hs.reshape(n_seq, t_new, d)
    return (y_prompt, y_sample, jnp.stack(outs["kp"]), jnp.stack(outs["vp"]), jnp.stack(outs["sp"]),
            jnp.stack(outs["ks"]), jnp.stack(outs["vs"]), jnp.stack(outs["ss"]))
```

```python
import functools

import jax
import jax.numpy as jnp
from jax import lax
from jax.experimental import pallas as pl
from jax.experimental.pallas import tpu as pltpu

F32 = jnp.float32
BF16 = jnp.bfloat16

RMS_EPS = 1e-6
HEAD_DIM = 128
N_HEADS = 8
GROUP_WIDTH = N_HEADS * HEAD_DIM
HG_CHUNK = 32
HG_SUB = 8
VMEM_LIMIT = 56 * 1024 * 1024


def _params(*sem):
    return pltpu.CompilerParams(dimension_semantics=sem, vmem_limit_bytes=VMEM_LIMIT)


def _rms_normalize(x, g):
    return x * lax.rsqrt(jnp.mean(x * x, axis=-1, keepdims=True) + RMS_EPS) * g


def _split_hi_lo(x):
    hi = x.astype(BF16)
    lo = (x - hi.astype(F32)).astype(BF16)
    return hi, lo


def _dot_nt(a, b):
    return lax.dot_general(a, b, (((1,), (1,)), ((), ())), preferred_element_type=F32)


def _dot_tn(a, b):
    return lax.dot_general(a, b, (((0,), (0,)), ((), ())), preferred_element_type=F32)


def _sigmoid(x):
    return 1.0 / (1.0 + jnp.exp(-x))


def _silu(x):
    return x * _sigmoid(x)


def _inproj_kernel(x_ref, g_ref, w_ref, *refs):
    out_refs, xn_ref = refs[:-1], refs[-1]
    j = pl.program_id(1)

    @pl.when(j == 0)
    def _():
        xn_ref[...] = _rms_normalize(x_ref[...], g_ref[...]).astype(BF16)

    acc = jnp.dot(xn_ref[...], w_ref[...], preferred_element_type=F32)
    for n, o_ref in enumerate(out_refs):
        @pl.when(j == n)
        def _(o_ref=o_ref):
            o_ref[...] = acc.astype(o_ref.dtype)


_INPROJ_DTYPES = (BF16, F32, F32, BF16, F32, BF16, BF16)


def _inproj(x, g, w_bf16, tm):
    m, d = x.shape
    n_groups = w_bf16.shape[1] // GROUP_WIDTH
    assert n_groups == len(_INPROJ_DTYPES) and m % tm == 0
    return pl.pallas_call(
        _inproj_kernel,
        grid=(m // tm, n_groups),
        in_specs=[
            pl.BlockSpec((tm, d), lambda i, j: (i, 0)),
            pl.BlockSpec((1, d), lambda i, j: (0, 0)),
            pl.BlockSpec((d, GROUP_WIDTH), lambda i, j: (0, j)),
        ],
        out_specs=[pl.BlockSpec((tm, GROUP_WIDTH), lambda i, j: (i, 0)) for _ in _INPROJ_DTYPES],
        out_shape=[jax.ShapeDtypeStruct((m, GROUP_WIDTH), dt) for dt in _INPROJ_DTYPES],
        scratch_shapes=[pltpu.VMEM((tm, d), BF16)],
        compiler_params=_params("parallel", "arbitrary"),
        name="inproj",
    )(x, g, w_bf16)


def _suffix_matrix():
    r = lax.broadcasted_iota(jnp.int32, (2 * HEAD_DIM, 2 * HEAD_DIM), 0) % HEAD_DIM
    c = lax.broadcasted_iota(jnp.int32, (2 * HEAD_DIM, 2 * HEAD_DIM), 1)
    return jnp.where((c >= HEAD_DIM) | (r > c), 1.0, 0.0).astype(BF16)


LOG2_E = 1.4426950408889634


def _log_break(z2):
    l = jnp.log2(1.0 + jnp.exp2(-jnp.abs(z2)))
    log_beta = jnp.minimum(z2, 0.0) - l
    return log_beta, log_beta - z2


def _half_sums(log_keep, suffix_mat):
    hi, lo = _split_hi_lo(log_keep)
    rows, n = log_keep.shape[0], log_keep.shape[1] // HEAD_DIM
    stacked = []
    for s in range(n):
        sl = slice(s * HEAD_DIM, (s + 1) * HEAD_DIM)
        stacked.append(jnp.concatenate([hi[:, sl], lo[:, sl]], axis=-1))
    c = jnp.dot(jnp.concatenate(stacked, axis=0) if n > 1 else stacked[0], suffix_mat,
                preferred_element_type=F32)
    return [(c[s * rows:(s + 1) * rows, :HEAD_DIM], c[s * rows:(s + 1) * rows, HEAD_DIM:])
            for s in range(n)]


def _sb_block(z2, carry, suffix_mat, visible=None, keep=None):
    log_beta, log_keep = _log_break(z2)
    if keep is not None:
        log_keep = log_keep * keep
    if visible is not None:
        log_keep = jnp.where(visible, log_keep, 0.0)
    halves = _half_sums(log_keep, suffix_mat)
    after = []
    for suffix, total in reversed(halves):
        after.append(suffix + carry)
        carry = carry + total
    after = jnp.concatenate(after[::-1], axis=-1) if len(after) > 1 else after[0]
    w = jnp.exp2(log_beta + after)
    if keep is not None:
        w = w * keep
    if visible is not None:
        w = jnp.where(visible, w, 0.0)
    return w, carry


SB_HEADS_PER_STEP = 2


def _sb_prompt_kernel(bias_ref, q_ref, k_ref, v_ref, g_ref, o_ref, *, tq, tk):
    hb = pl.program_id(1)
    i = pl.program_id(2)
    scale2 = HEAD_DIM ** -0.5 * LOG2_E
    suffix_mat = _suffix_matrix()
    n_diag = tq // tk
    heads = [slice(n * HEAD_DIM, (n + 1) * HEAD_DIM) for n in range(SB_HEADS_PER_STEP)]
    qs = [q_ref[:, sl] for sl in heads]
    bias2 = [bias_ref[hb * SB_HEADS_PER_STEP + n] * LOG2_E for n in range(SB_HEADS_PER_STEP)]

    def block(j, state, visible):
        start = pl.multiple_of(j * tk, tk)
        new = []
        for n, sl in enumerate(heads):
            acc, carry = state[n]
            kb = (k_ref[pl.ds(start, tk), sl] * scale2).astype(BF16)
            vb = v_ref[pl.ds(start, tk), sl].astype(BF16)
            w, carry = _sb_block(_dot_nt(qs[n], kb) + bias2[n], carry, suffix_mat, visible)
            new.append((acc + jnp.dot(w.astype(BF16), vb, preferred_element_type=F32), carry))
        return tuple(new)

    row = lax.broadcasted_iota(jnp.int32, (tq, tk), 0)
    col = lax.broadcasted_iota(jnp.int32, (tq, tk), 1)
    zeros = jnp.zeros((tq, HEAD_DIM), F32)
    state = tuple((zeros, zeros) for _ in heads)
    for d in reversed(range(n_diag)):
        state = block(i * n_diag + d, state, col + d * tk < row)

    state = lax.fori_loop(0, i * n_diag, lambda jj, st: block(i * n_diag - 1 - jj, st, None), state)
    for n, sl in enumerate(heads):
        o_ref[:, sl] = _rms_normalize(state[n][0], g_ref[:, sl]).astype(o_ref.dtype)


def _sb_prompt(q, k, v, bias, g, batch, seq, tq, tk):
    nq = seq // tq
    width = SB_HEADS_PER_STEP * HEAD_DIM
    assert tq % tk == 0 and tk % HEAD_DIM == 0 and N_HEADS % SB_HEADS_PER_STEP == 0
    return pl.pallas_call(
        functools.partial(_sb_prompt_kernel, tq=tq, tk=tk),
        grid=(batch, N_HEADS // SB_HEADS_PER_STEP, nq),
        in_specs=[
            pl.BlockSpec(memory_space=pltpu.SMEM),
            pl.BlockSpec((tq, width), lambda b, h, i: (b * nq + i, h)),
            pl.BlockSpec((seq, width), lambda b, h, i: (b, h)),
            pl.BlockSpec((seq, width), lambda b, h, i: (b, h)),
            pl.BlockSpec((1, width), lambda b, h, i: (0, h)),
        ],
        out_specs=pl.BlockSpec((tq, width), lambda b, h, i: (b * nq + i, h)),
        out_shape=jax.ShapeDtypeStruct((batch * seq, GROUP_WIDTH), BF16),
        compiler_params=_params("parallel", "parallel", "arbitrary"),
        name="sb_prompt",
    )(bias, q, k, v, g)


def _sb_decode_kernel(pt_ref, q_ref, bias_ref, kn_ref, vn_ref, *refs, t_new, pages_per_step):
    del pt_ref
    k_refs, v_refs = refs[:pages_per_step], refs[pages_per_step:2 * pages_per_step]
    g_ref, o_ref, acc_ref, carry_ref = refs[2 * pages_per_step:]
    p = pl.program_id(1)
    rows = N_HEADS * t_new
    page_cols = k_refs[0].shape[2] * N_HEADS
    scale2 = HEAD_DIM ** -0.5 * LOG2_E
    suffix_mat = _suffix_matrix()
    q = q_ref[0]

    def tiled(x, n):
        return jnp.concatenate([x] * n, axis=-1) if n > 1 else x

    def attend(keys, values, acc, carry, **mask):
        n = keys.shape[0] // HEAD_DIM
        z2 = _dot_nt(q, keys.astype(BF16)) * scale2 + tiled(bias_ref[...] * LOG2_E, n)
        w, carry = _sb_block(z2, carry, suffix_mat, **mask)
        return acc + jnp.dot(w.astype(BF16), values.astype(BF16), preferred_element_type=F32), carry

    r = lax.broadcasted_iota(jnp.int32, (rows, HEAD_DIM), 0)
    c = lax.broadcasted_iota(jnp.int32, (rows, HEAD_DIM), 1)
    heads_agree = (c % N_HEADS) == (r // t_new)

    @pl.when(p == 0)
    def _():
        pad = jnp.zeros((HEAD_DIM - rows, HEAD_DIM), F32)
        kn = jnp.concatenate([kn_ref[0], pad], axis=0)
        vn = jnp.concatenate([vn_ref[0], pad], axis=0)
        zeros = jnp.zeros((rows, HEAD_DIM), F32)
        acc, carry = attend(kn, vn, zeros, zeros,
                            visible=heads_agree & (c // N_HEADS < r % t_new))
        acc_ref[...] = acc
        carry_ref[...] = carry

    keys = jnp.concatenate([kr[0, 0].reshape(page_cols, HEAD_DIM) for kr in k_refs[::-1]], axis=0)
    values = jnp.concatenate([vr[0, 0].reshape(page_cols, HEAD_DIM) for vr in v_refs[::-1]], axis=0)
    keep = tiled(jnp.where(heads_agree, 1.0, 0.0), page_cols * pages_per_step // HEAD_DIM)
    acc, carry = attend(keys, values, acc_ref[...], carry_ref[...], keep=keep)
    acc_ref[...] = acc
    carry_ref[...] = carry

    @pl.when(p == pl.num_programs(1) - 1)
    def _():
        o_ref[0] = _rms_normalize(acc, g_ref[...])


def _sb_decode(q_rows, bias_rows, k_new, v_new, cache_k, cache_v, page_table, g_rows, layer, t_new):
    n_seq, rows, _ = q_rows.shape
    n_pages = page_table.shape[1]
    page = cache_k.shape[2]
    assert cache_k.shape[3:] == (N_HEADS, HEAD_DIM) and rows <= HEAD_DIM
    pages_per_step = next(n for n in (4, 2, 1) if n_pages % n == 0)
    row_spec = pl.BlockSpec((1, rows, HEAD_DIM), lambda s, p, pt: (s, 0, 0))
    const_spec = pl.BlockSpec((rows, HEAD_DIM), lambda s, p, pt: (0, 0))

    def page_spec(i):
        def index_map(s, p, pt):
            return (layer, pt[s * n_pages + n_pages - 1 - (p * pages_per_step + i)], 0, 0, 0)
        return pl.BlockSpec((1, 1, page, N_HEADS, HEAD_DIM), index_map)

    page_specs = [page_spec(i) for i in range(pages_per_step)]
    return pl.pallas_call(
        functools.partial(_sb_decode_kernel, t_new=t_new, pages_per_step=pages_per_step),
        grid_spec=pltpu.PrefetchScalarGridSpec(
            num_scalar_prefetch=1,
            grid=(n_seq, n_pages // pages_per_step),
            in_specs=[row_spec, const_spec, row_spec, row_spec] + page_specs + page_specs + [const_spec],
            out_specs=row_spec,
            scratch_shapes=[pltpu.VMEM((rows, HEAD_DIM), F32), pltpu.VMEM((rows, HEAD_DIM), F32)],
        ),
        out_shape=jax.ShapeDtypeStruct((n_seq, rows, HEAD_DIM), F32),
        compiler_params=_params("parallel", "arbitrary"),
        name="sb_decode",
    )(page_table.reshape(-1), q_rows, bias_rows, k_new, v_new,
      *([cache_k] * pages_per_step), *([cache_v] * pages_per_step), g_rows)


def _lower_bound(logits, layer):
    e = jnp.exp(logits - jnp.max(logits, axis=0, keepdims=True))
    return jnp.sum(e[:layer + 1], axis=0, keepdims=True) / jnp.sum(e, axis=0, keepdims=True)


def _hgrn_gates(hq, hf, lb):
    f = lb + (1.0 - lb) * _sigmoid(hf)
    return _silu(hq), 1.0 - f, jnp.log(f)


def _block_tril(n, blk):
    r = lax.broadcasted_iota(jnp.int32, (n, n), 0)
    c = lax.broadcasted_iota(jnp.int32, (n, n), 1)
    return jnp.where((c <= r) & (r // blk == c // blk), 1.0, 0.0).astype(BF16)


def _hgrn_prompt_kernel(hq_ref, hf_ref, hi_ref, hg_ref, lbl_ref, g_ref, o_ref, s_ref, st_ref,
                        *, tc, layer):
    t = pl.program_id(2)
    n_chunks = tc // HG_CHUNK
    n_sub = HG_CHUNK // HG_SUB

    @pl.when(t == 0)
    def _():
        st_ref[...] = jnp.zeros_like(st_ref)

    lb = _lower_bound(lbl_ref[...], layer)
    q, kk, g = _hgrn_gates(hq_ref[...].astype(F32), hf_ref[...], lb)
    v = hi_ref[...].astype(F32)
    v_bf = v.astype(BF16)

    g_hi, g_lo = _split_hi_lo(g)
    tril_chunk = _block_tril(tc, HG_CHUNK)
    tril_sub = _block_tril(tc, HG_SUB)
    b = (jnp.dot(tril_chunk, g_hi, preferred_element_type=F32)
         + jnp.dot(tril_chunk, g_lo, preferred_element_type=F32))
    c = (jnp.dot(tril_sub, g_hi, preferred_element_type=F32)
         + jnp.dot(tril_sub, g_lo, preferred_element_type=F32))

    sub_shape = (tc // HG_SUB, HG_SUB, HEAD_DIM)
    r_sub = lax.broadcasted_iota(jnp.int32, sub_shape, 1)
    q3, k3, c3, v3 = (a.reshape(sub_shape) for a in (q, kk, c, v))
    prods, shifted_v = [(q3 * k3)], [v3]
    for delta in range(1, HG_SUB):
        ks = pltpu.roll(k3, delta, axis=1)
        cs = pltpu.roll(c3, delta, axis=1)
        decay = jnp.exp(c3 - cs)
        prods.append(jnp.where(r_sub >= delta, q3 * ks * decay, 0.0))
        shifted_v.append(pltpu.roll(v3, delta, axis=1))
    ones = jnp.ones((HEAD_DIM, HEAD_DIM), BF16)
    prods = jnp.concatenate(prods, axis=0).reshape(HG_SUB * tc, HEAD_DIM).astype(BF16)
    scores = jnp.dot(prods, ones, preferred_element_type=F32)
    o = jnp.zeros((tc, HEAD_DIM), F32)
    for delta in range(HG_SUB):
        o = o + scores[delta * tc:(delta + 1) * tc] * shifted_v[delta].reshape(tc, HEAD_DIM)

    chunk_shape = (n_chunks, HG_CHUNK, HEAD_DIM)
    r_chunk = lax.broadcasted_iota(jnp.int32, chunk_shape, 1)
    b3 = b.reshape(chunk_shape)
    kk3 = kk.reshape(chunk_shape)
    q_sub = (q * jnp.exp(c)).reshape(chunk_shape)
    q_cat, k_cat = [], []
    for i in range(1, n_sub):
        lo_row = i * HG_SUB
        b_start = b3[:, lo_row - 1:lo_row, :]
        k_cat.append(jnp.where(r_chunk < lo_row,
                               kk3 * jnp.exp(jnp.minimum(b_start - b3, 0.0)), 0.0).astype(BF16))
        q_cat.append(jnp.where((r_chunk >= lo_row) & (r_chunk < lo_row + HG_SUB),
                               q_sub, 0.0).astype(BF16))
    q_cat = jnp.concatenate(q_cat, axis=-1)
    k_cat = jnp.concatenate(k_cat, axis=-1)
    a_off = jnp.einsum('ctk,csk->cts', q_cat, k_cat, preferred_element_type=F32)
    o_off = jnp.einsum('cts,csv->ctv', a_off.astype(BF16), v_bf.reshape(chunk_shape),
                       preferred_element_type=F32)
    o = o + o_off.reshape(tc, HEAD_DIM)

    q_in = (q * jnp.exp(b)).astype(BF16)
    st = st_ref[...]
    o_inter = []
    for ci in range(n_chunks):
        rows = slice(ci * HG_CHUNK, (ci + 1) * HG_CHUNK)
        b_c = b[rows]
        b_last = b_c[HG_CHUNK - 1:HG_CHUNK]
        o_inter.append(_dot_nt(q_in[rows], st.astype(BF16)))
        k_out = (kk[rows] * jnp.exp(b_last - b_c)).astype(BF16)
        st = st * jnp.exp(b_last) + _dot_tn(v_bf[rows], k_out)
    st_ref[...] = st
    o = o + jnp.concatenate(o_inter, axis=0)

    o_ref[...] = (_rms_normalize(o, g_ref[...]) * _silu(hg_ref[...].astype(F32))).astype(o_ref.dtype)

    @pl.when(t == pl.num_programs(2) - 1)
    def _():
        s_ref[0, 0] = st.T


def _hgrn_prompt(hq, hf, hi, hg, lb_logits, g, batch, seq, tc, layer):
    nt = seq // tc
    n_layers = lb_logits.shape[0]
    tok = lambda b, h, t: (b * nt + t, h)
    return pl.pallas_call(
        functools.partial(_hgrn_prompt_kernel, tc=tc, layer=layer),
        grid=(batch, N_HEADS, nt),
        in_specs=[
            pl.BlockSpec((tc, HEAD_DIM), tok),
            pl.BlockSpec((tc, HEAD_DIM), tok),
            pl.BlockSpec((tc, HEAD_DIM), tok),
            pl.BlockSpec((tc, HEAD_DIM), tok),
            pl.BlockSpec((n_layers, HEAD_DIM), lambda b, h, t: (0, h)),
            pl.BlockSpec((1, HEAD_DIM), lambda b, h, t: (0, h)),
        ],
        out_specs=[
            pl.BlockSpec((tc, HEAD_DIM), tok),
            pl.BlockSpec((1, 1, HEAD_DIM, HEAD_DIM), lambda b, h, t: (b, h, 0, 0)),
        ],
        out_shape=[
            jax.ShapeDtypeStruct((batch * seq, GROUP_WIDTH), BF16),
            jax.ShapeDtypeStruct((batch, N_HEADS, HEAD_DIM, HEAD_DIM), F32),
        ],
        scratch_shapes=[pltpu.VMEM((HEAD_DIM, HEAD_DIM), F32)],
        compiler_params=_params("parallel", "parallel", "arbitrary"),
        name="hgrn_prompt",
    )(hq, hf, hi, hg, lb_logits, g)


def _hgrn_decode_kernel(hq_ref, hf_ref, hi_ref, hg_ref, s_ref, lbl_ref, g_ref, o_ref, so_ref,
                        *, t_new, layer):
    for h in range(N_HEADS):
        sl = slice(h * HEAD_DIM, (h + 1) * HEAD_DIM)
        lb = _lower_bound(lbl_ref[:, sl], layer)
        q, kk, g = _hgrn_gates(hq_ref[0, :, sl].astype(F32), hf_ref[0, :, sl], lb)
        v = hi_ref[0, :, sl].astype(F32)
        s = s_ref[0, 0, h]

        b_rows = [g[0:1]]
        for t in range(1, t_new):
            b_rows.append(b_rows[-1] + g[t:t + 1])
        b_last = b_rows[-1]

        o_rows = []
        for t in range(t_new):
            o_t = jnp.dot((q[t:t + 1] * jnp.exp(b_rows[t])).astype(BF16), s.astype(BF16),
                          preferred_element_type=F32)
            for u in range(t + 1):
                a = jnp.sum(q[t:t + 1] * kk[u:u + 1] * jnp.exp(b_rows[t] - b_rows[u]),
                            axis=-1, keepdims=True)
                o_t = o_t + a * v[u:u + 1]
            o_rows.append(o_t)
        o = jnp.concatenate(o_rows, axis=0)
        o_ref[0, :, sl] = _rms_normalize(o, g_ref[:, sl]) * _silu(hg_ref[0, :, sl].astype(F32))

        cols = [kk[u:u + 1] * jnp.exp(b_last - b_rows[u]) for u in range(t_new)] + [jnp.exp(b_last)]
        cols += [jnp.zeros_like(b_last)] * (HG_SUB - len(cols))
        cols_t = jnp.concatenate(cols, axis=0).T
        s_new = s * cols_t[:, t_new:t_new + 1]
        for u in range(t_new):
            s_new = s_new + cols_t[:, u:u + 1] * v[u:u + 1]
        so_ref[0, 0, h] = s_new


def _hgrn_decode(hq, hf, hi, hg, state, lb_logits, g, layer):
    n_seq, t_new, _ = hq.shape
    assert t_new < HG_SUB
    n_layers = lb_logits.shape[0]
    row_spec = pl.BlockSpec((1, t_new, GROUP_WIDTH), lambda s: (s, 0, 0))
    state_spec = pl.BlockSpec((1, 1, N_HEADS, HEAD_DIM, HEAD_DIM), lambda s: (layer, s, 0, 0, 0))
    return pl.pallas_call(
        functools.partial(_hgrn_decode_kernel, t_new=t_new, layer=layer),
        grid=(n_seq,),
        in_specs=[
            row_spec, row_spec, row_spec, row_spec, state_spec,
            pl.BlockSpec((n_layers, GROUP_WIDTH), lambda s: (0, 0)),
            pl.BlockSpec((1, GROUP_WIDTH), lambda s: (0, 0)),
        ],
        out_specs=[
            row_spec,
            pl.BlockSpec((1, 1, N_HEADS, HEAD_DIM, HEAD_DIM), lambda s: (0, s, 0, 0, 0)),
        ],
        out_shape=[
            jax.ShapeDtypeStruct((n_seq, t_new, GROUP_WIDTH), F32),
            jax.ShapeDtypeStruct((1, n_seq, N_HEADS, HEAD_DIM, HEAD_DIM), F32),
        ],
        compiler_params=_params("parallel"),
        name="hgrn_decode",
    )(hq, hf, hi, hg, state, lb_logits, g)


def _outproj_kernel(sb_ref, hg_ref, w_ref, x_ref, o_ref):
    half = sb_ref.shape[1]
    acc = jnp.dot(sb_ref[...].astype(BF16), w_ref[:half, :], preferred_element_type=F32)
    acc = acc + jnp.dot(hg_ref[...].astype(BF16), w_ref[half:, :], preferred_element_type=F32)
    o_ref[...] = x_ref[...] + acc


def _outproj(sb, hg, w_bf16, x, tm, tn):
    m, d = x.shape
    width = sb.shape[1]
    return pl.pallas_call(
        _outproj_kernel,
        grid=(m // tm, d // tn),
        in_specs=[
            pl.BlockSpec((tm, width), lambda i, j: (i, 0)),
            pl.BlockSpec((tm, width), lambda i, j: (i, 0)),
            pl.BlockSpec((2 * width, tn), lambda i, j: (0, j)),
            pl.BlockSpec((tm, tn), lambda i, j: (i, j)),
        ],
        out_specs=pl.BlockSpec((tm, tn), lambda i, j: (i, j)),
        out_shape=jax.ShapeDtypeStruct((m, d), F32),
        compiler_params=_params("parallel", "arbitrary"),
        name="outproj",
    )(sb, hg, w_bf16, x)


def _mlp_kernel(h_ref, g2_ref, wu_ref, wd_ref, gf_ref, o_ref, hn_ref, acc_ref, *, final_norm):
    j = pl.program_id(1)

    @pl.when(j == 0)
    def _():
        hn_ref[...] = _rms_normalize(h_ref[...], g2_ref[...]).astype(BF16)
        acc_ref[...] = jnp.zeros_like(acc_ref)

    u = jnp.dot(hn_ref[...], wu_ref[...], preferred_element_type=F32)
    a = jnp.square(jnp.maximum(u, 0.0)).astype(BF16)
    acc_ref[...] += jnp.dot(a, wd_ref[...], preferred_element_type=F32)

    @pl.when(j == pl.num_programs(1) - 1)
    def _():
        y = h_ref[...] + acc_ref[...]
        o_ref[...] = _rms_normalize(y, gf_ref[...]) if final_norm else y


def _mlp(h, g2, wu_bf16, wd_bf16, gf, tm, tf, final_norm):
    m, d = h.shape
    f = wu_bf16.shape[1]
    return pl.pallas_call(
        functools.partial(_mlp_kernel, final_norm=final_norm),
        grid=(m // tm, f // tf),
        in_specs=[
            pl.BlockSpec((tm, d), lambda i, j: (i, 0)),
            pl.BlockSpec((1, d), lambda i, j: (0, 0)),
            pl.BlockSpec((d, tf), lambda i, j: (0, j)),
            pl.BlockSpec((tf, d), lambda i, j: (j, 0)),
            pl.BlockSpec((1, d), lambda i, j: (0, 0)),
        ],
        out_specs=pl.BlockSpec((tm, d), lambda i, j: (i, 0)),
        out_shape=jax.ShapeDtypeStruct((m, d), F32),
        scratch_shapes=[pltpu.VMEM((tm, d), BF16), pltpu.VMEM((tm, d), F32)],
        compiler_params=_params("parallel", "arbitrary"),
        name="mlp",
    )(h, g2, wu_bf16, wd_bf16, gf)


def _row_tile(m, target):
    return target if m % target == 0 else m


def kernel(x_prompt, x_sample, cache_k, cache_v, state_hgrn, page_table, norm1_g, w_in, sb_bias,
           sb_norm_g, hg_norm_g, hg_lb_logits, w_out, norm2_g, w_up, w_down, final_norm_g):
    batch, seq, d = x_prompt.shape
    n_seq, t_new, _ = x_sample.shape
    depth = w_in.shape[0]
    hp = x_prompt.reshape(batch * seq, d)
    hs = x_sample.reshape(n_seq * t_new, d)
    tm_p = _row_tile(batch * seq, 512)
    tm_s = _row_tile(n_seq * t_new, 512)
    tq = _row_tile(seq, 512)
    tk_sb = _row_tile(tq, 256)
    tc = _row_tile(seq, 256)
    gf = final_norm_g.reshape(1, d)

    outs = {k: [] for k in ("kp", "vp", "sp", "ks", "vs", "ss")}
    for l in range(depth):
        g1 = norm1_g[l].reshape(1, d)
        g2 = norm2_g[l].reshape(1, d)
        sbg = sb_norm_g[l].reshape(1, GROUP_WIDTH)
        hgg = hg_norm_g[l].reshape(1, GROUP_WIDTH)
        w_in_l = w_in[l].astype(BF16)
        w_out_l = w_out[l].astype(BF16)
        w_up_l = w_up[l].astype(BF16)
        w_down_l = w_down[l].astype(BF16)
        last = l == depth - 1

        sq, sk, sv, hq, hf, hi, hg = _inproj(hp, g1, w_in_l, tm_p)
        sb = _sb_prompt(sq, sk, sv, sb_bias[l], sbg, batch, seq, tq, tk_sb)
        hgo, s_p = _hgrn_prompt(hq, hf, hi, hg, hg_lb_logits, hgg, batch, seq, tc, l)
        hmid = _outproj(sb, hgo, w_out_l, hp, tm_p, 1024)
        hp = _mlp(hmid, g2, w_up_l, w_down_l, gf, tm_p, 1024, last)
        outs["kp"].append(sk.reshape(batch, seq, N_HEADS, HEAD_DIM))
        outs["vp"].append(sv.reshape(batch, seq, N_HEADS, HEAD_DIM))
        outs["sp"].append(s_p)

        tq_, tk, tv, uq, uf, ui, ug = (a.reshape(n_seq, t_new, GROUP_WIDTH)
                                       for a in _inproj(hs, g1, w_in_l, tm_s))
        rows = N_HEADS * t_new
        sbg_rows = jnp.repeat(sbg.reshape(N_HEADS, HEAD_DIM), t_new, axis=0)
        bias_rows = jnp.broadcast_to(jnp.repeat(sb_bias[l], t_new)[:, None], (rows, HEAD_DIM))
        q_rows = tq_.reshape(n_seq, t_new, N_HEADS, HEAD_DIM).transpose(0, 2, 1, 3)
        sb2 = _sb_decode(q_rows.reshape(n_seq, rows, HEAD_DIM), bias_rows,
                         tk.reshape(n_seq, rows, HEAD_DIM), tv.reshape(n_seq, rows, HEAD_DIM),
                         cache_k, cache_v, page_table, sbg_rows, l, t_new)
        sb2 = sb2.reshape(n_seq, N_HEADS, t_new, HEAD_DIM).transpose(0, 2, 1, 3)
        sb2 = sb2.reshape(n_seq * t_new, GROUP_WIDTH)
        hgo2, s_s = _hgrn_decode(uq, uf, ui, ug, state_hgrn, hg_lb_logits, hgg, l)
        hmid2 = _outproj(sb2, hgo2.reshape(n_seq * t_new, GROUP_WIDTH), w_out_l, hs, tm_s, 1024)
        hs = _mlp(hmid2, g2, w_up_l, w_down_l, gf, tm_s, 1024, last)
        outs["ks"].append(tk.reshape(n_seq, t_new, N_HEADS, HEAD_DIM))
        outs["vs"].append(tv.reshape(n_seq, t_new, N_HEADS, HEAD_DIM))
        outs["ss"].append(s_s[0])

    y_prompt = hp.reshape(batch, seq, d)
    y_sample = hs.reshape(n_seq, t_new, d)
    return (y_prompt, y_sample, jnp.stack(outs["kp"]), jnp.stack(outs["vp"]), jnp.stack(outs["sp"]),
            jnp.stack(outs["ks"]), jnp.stack(outs["vs"]), jnp.stack(outs["ss"]))
```

```python
import functools

import jax
import jax.numpy as jnp
from jax import lax
from jax.experimental import pallas as pl
from jax.experimental.pallas import tpu as pltpu

F32 = jnp.float32
BF16 = jnp.bfloat16

RMS_EPS = 1e-6
HEAD_DIM = 128
N_HEADS = 8
GROUP_WIDTH = N_HEADS * HEAD_DIM
HG_CHUNK = 32
HG_SUB = 8
VMEM_LIMIT = 56 * 1024 * 1024


def _params(*sem):
    return pltpu.CompilerParams(dimension_semantics=sem, vmem_limit_bytes=VMEM_LIMIT)


def _rms_normalize(x, g):
    return x * lax.rsqrt(jnp.mean(x * x, axis=-1, keepdims=True) + RMS_EPS) * g


def _split_hi_lo(x):
    hi = x.astype(BF16)
    lo = (x - hi.astype(F32)).astype(BF16)
    return hi, lo


def _dot_nt(a, b):
    return lax.dot_general(a, b, (((1,), (1,)), ((), ())), preferred_element_type=F32)


def _dot_tn(a, b):
    return lax.dot_general(a, b, (((0,), (0,)), ((), ())), preferred_element_type=F32)


def _sigmoid(x):
    return 1.0 / (1.0 + jnp.exp(-x))


def _silu(x):
    return x * _sigmoid(x)


INPROJ_F32_GROUPS = (1, 2, 4)
INPROJ_BF16_GROUPS = (0, 3, 5, 6)


def _inproj_kernel(order_ref, x_ref, g_ref, w_ref, *refs):
    del order_ref
    f32_refs, narrow_ref, xn_ref = refs[:-2], refs[-2], refs[-1]
    j = pl.program_id(1)

    @pl.when(j == 0)
    def _():
        xn_ref[...] = _rms_normalize(x_ref[...], g_ref[...]).astype(BF16)

    for n, o_ref in enumerate(f32_refs):
        @pl.when(j == n)
        def _(o_ref=o_ref):
            o_ref[...] = jnp.dot(xn_ref[...], w_ref[...], preferred_element_type=F32)

    @pl.when(j >= len(f32_refs))
    def _():
        narrow_ref[...] = jnp.dot(xn_ref[...], w_ref[...],
                                  preferred_element_type=F32).astype(BF16)


def _inproj(x, g, w_bf16, tm):
    m, d = x.shape
    n_wide, n_narrow = len(INPROJ_F32_GROUPS), len(INPROJ_BF16_GROUPS)
    assert w_bf16.shape[1] == (n_wide + n_narrow) * GROUP_WIDTH and m % tm == 0
    order = jnp.array(INPROJ_F32_GROUPS + INPROJ_BF16_GROUPS, jnp.int32)
    f32_spec = pl.BlockSpec((tm, GROUP_WIDTH), lambda i, j, order: (i, 0))
    return pl.pallas_call(
        _inproj_kernel,
        grid_spec=pltpu.PrefetchScalarGridSpec(
            num_scalar_prefetch=1,
            grid=(m // tm, n_wide + n_narrow),
            in_specs=[
                pl.BlockSpec((tm, d), lambda i, j, order: (i, 0)),
                pl.BlockSpec((1, d), lambda i, j, order: (0, 0)),
                pl.BlockSpec((d, GROUP_WIDTH), lambda i, j, order: (0, order[j])),
            ],
            out_specs=[f32_spec] * n_wide + [
                pl.BlockSpec((None, tm, GROUP_WIDTH),
                             lambda i, j, order: (jnp.maximum(j - n_wide, 0), i, 0)),
            ],
            scratch_shapes=[pltpu.VMEM((tm, d), BF16)],
        ),
        out_shape=[jax.ShapeDtypeStruct((m, GROUP_WIDTH), F32)] * n_wide
        + [jax.ShapeDtypeStruct((n_narrow, m, GROUP_WIDTH), BF16)],
        compiler_params=_params("parallel", "arbitrary"),
        name="inproj",
    )(order, x, g, w_bf16)


def _suffix_matrix():
    r = lax.broadcasted_iota(jnp.int32, (2 * HEAD_DIM, 2 * HEAD_DIM), 0) % HEAD_DIM
    c = lax.broadcasted_iota(jnp.int32, (2 * HEAD_DIM, 2 * HEAD_DIM), 1)
    return jnp.where((c >= HEAD_DIM) | (r > c), 1.0, 0.0).astype(BF16)


LOG2_E = 1.4426950408889634


def _log_break(z2):
    l = jnp.log2(1.0 + jnp.exp2(-jnp.abs(z2)))
    log_beta = jnp.minimum(z2, 0.0) - l
    return log_beta, log_beta - z2


def _half_sums(log_keep, suffix_mat):
    hi, lo = _split_hi_lo(log_keep)
    rows, n = log_keep.shape[0], log_keep.shape[1] // HEAD_DIM
    stacked = []
    for s in range(n):
        sl = slice(s * HEAD_DIM, (s + 1) * HEAD_DIM)
        stacked.append(jnp.concatenate([hi[:, sl], lo[:, sl]], axis=-1))
    c = jnp.dot(jnp.concatenate(stacked, axis=0) if n > 1 else stacked[0], suffix_mat,
                preferred_element_type=F32)
    return [(c[s * rows:(s + 1) * rows, :HEAD_DIM], c[s * rows:(s + 1) * rows, HEAD_DIM:])
            for s in range(n)]


def _sb_sums(z2, suffix_mat, visible=None, keep=None):
    log_beta, log_keep = _log_break(z2)
    if keep is not None:
        log_keep = log_keep * keep
    if visible is not None:
        log_keep = jnp.where(visible, log_keep, 0.0)
    return log_beta, _half_sums(log_keep, suffix_mat)


def _sb_weights(log_beta, halves, carry, visible=None, keep=None):
    after = []
    for suffix, total in reversed(halves):
        after.append(suffix + carry)
        carry = carry + total
    after = jnp.concatenate(after[::-1], axis=-1) if len(after) > 1 else after[0]
    w = jnp.exp2(log_beta + after)
    if keep is not None:
        w = w * keep
    if visible is not None:
        w = jnp.where(visible, w, 0.0)
    return w, carry


def _sb_block(z2, carry, suffix_mat, **mask):
    log_beta, halves = _sb_sums(z2, suffix_mat, **mask)
    return _sb_weights(log_beta, halves, carry, **mask)


SB_HEADS_PER_STEP = 4


def _sb_prompt_kernel(bias_ref, q_ref, k_ref, v_ref, g_ref, o_ref, *, tq, tk):
    hb = pl.program_id(1)
    i = pl.program_id(2)
    scale2 = HEAD_DIM ** -0.5 * LOG2_E
    suffix_mat = _suffix_matrix()
    n_diag = tq // tk
    heads = [slice(n * HEAD_DIM, (n + 1) * HEAD_DIM) for n in range(SB_HEADS_PER_STEP)]
    qs = [q_ref[:, sl] for sl in heads]
    bias2 = [bias_ref[hb * SB_HEADS_PER_STEP + n] * LOG2_E for n in range(SB_HEADS_PER_STEP)]

    def logits(j):
        start = pl.multiple_of(j * tk, tk)
        return tuple(_dot_nt(qs[n], (k_ref[pl.ds(start, tk), sl] * scale2).astype(BF16))
                     for n, sl in enumerate(heads))

    def block(j, z, state, visible):
        start = pl.multiple_of(j * tk, tk)
        sums = [_sb_sums(z[n] + bias2[n], suffix_mat, visible) for n in range(len(heads))]
        new = []
        for n, sl in enumerate(heads):
            acc, carry = state[n]
            w, carry = _sb_weights(*sums[n], carry, visible)
            vb = v_ref[pl.ds(start, tk), sl].astype(BF16)
            new.append((acc + jnp.dot(w.astype(BF16), vb, preferred_element_type=F32), carry))
        return tuple(new)

    row = lax.broadcasted_iota(jnp.int32, (tq, tk), 0)
    col = lax.broadcasted_iota(jnp.int32, (tq, tk), 1)
    zeros = jnp.zeros((tq, HEAD_DIM), F32)
    state = tuple((zeros, zeros) for _ in heads)
    for d in reversed(range(n_diag)):
        state = block(i * n_diag + d, logits(i * n_diag + d), state, col + d * tk < row)

    n_before = i * n_diag

    def body(jj, state):
        j = n_before - 1 - jj
        return block(j, logits(j), state, None)

    state = lax.fori_loop(0, n_before, body, state)
    for n, sl in enumerate(heads):
        o_ref[:, sl] = _rms_normalize(state[n][0], g_ref[:, sl]).astype(o_ref.dtype)


def _narrow_spec(rows, width, group, index_map):
    slot = INPROJ_BF16_GROUPS.index(group)
    return pl.BlockSpec((None, rows, width), lambda *idx: (slot,) + tuple(index_map(*idx)))


def _sb_prompt(narrow, k, v, bias, g, batch, seq, tq, tk):
    nq = seq // tq
    width = SB_HEADS_PER_STEP * HEAD_DIM
    assert tq % tk == 0 and tk % HEAD_DIM == 0 and N_HEADS % SB_HEADS_PER_STEP == 0
    return pl.pallas_call(
        functools.partial(_sb_prompt_kernel, tq=tq, tk=tk),
        grid=(batch, N_HEADS // SB_HEADS_PER_STEP, nq),
        in_specs=[
            pl.BlockSpec(memory_space=pltpu.SMEM),
            _narrow_spec(tq, width, 0, lambda b, h, i: (b * nq + i, h)),
            pl.BlockSpec((seq, width), lambda b, h, i: (b, h)),
            pl.BlockSpec((seq, width), lambda b, h, i: (b, h)),
            pl.BlockSpec((1, width), lambda b, h, i: (0, h)),
        ],
        out_specs=pl.BlockSpec((tq, width), lambda b, h, i: (b * nq + i, h)),
        out_shape=jax.ShapeDtypeStruct((batch * seq, GROUP_WIDTH), BF16),
        compiler_params=_params("parallel", "parallel", "arbitrary"),
        name="sb_prompt",
    )(bias, narrow, k, v, g)


def _sb_decode_kernel(pt_ref, q_ref, bias_ref, kn_ref, vn_ref, *refs, t_new, pages_per_step,
                      seqs_per_step):
    del pt_ref
    n_page_refs = seqs_per_step * pages_per_step
    k_refs, v_refs = refs[:n_page_refs], refs[n_page_refs:2 * n_page_refs]
    g_ref, o_ref, acc_ref, carry_ref = refs[2 * n_page_refs:]
    p = pl.program_id(1)
    rows = N_HEADS * t_new
    page_cols = k_refs[0].shape[2] * N_HEADS
    scale2 = HEAD_DIM ** -0.5 * LOG2_E
    suffix_mat = _suffix_matrix()
    seqs = range(seqs_per_step)

    def tiled(x, n):
        return jnp.concatenate([x] * n, axis=-1) if n > 1 else x

    def attend(keys, values, acc, carry, **mask):
        n = keys[0].shape[0] // HEAD_DIM
        bias2 = tiled(bias_ref[...] * LOG2_E, n)
        z2 = [_dot_nt(q_ref[s], keys[s].astype(BF16)) * scale2 + bias2 for s in seqs]
        sums = [_sb_sums(z2[s], suffix_mat, **mask) for s in seqs]
        out = []
        for s in seqs:
            w, new_carry = _sb_weights(*sums[s], carry[s], **mask)
            out.append((acc[s] + jnp.dot(w.astype(BF16), values[s].astype(BF16),
                                         preferred_element_type=F32), new_carry))
        return [o[0] for o in out], [o[1] for o in out]

    r = lax.broadcasted_iota(jnp.int32, (rows, HEAD_DIM), 0)
    c = lax.broadcasted_iota(jnp.int32, (rows, HEAD_DIM), 1)
    heads_agree = (c % N_HEADS) == (r // t_new)

    @pl.when(p == 0)
    def _():
        pad = jnp.zeros((HEAD_DIM - rows, HEAD_DIM), F32)
        kn = [jnp.concatenate([kn_ref[s], pad], axis=0) for s in seqs]
        vn = [jnp.concatenate([vn_ref[s], pad], axis=0) for s in seqs]
        zeros = [jnp.zeros((rows, HEAD_DIM), F32)] * seqs_per_step
        acc, carry = attend(kn, vn, zeros, zeros, visible=heads_agree & (c // N_HEADS < r % t_new))
        for s in seqs:
            acc_ref[s] = acc[s]
            carry_ref[s] = carry[s]

    def pages(page_refs, s):
        mine = page_refs[s * pages_per_step:(s + 1) * pages_per_step]
        return jnp.concatenate([ref[0, 0].reshape(page_cols, HEAD_DIM) for ref in mine[::-1]], axis=0)

    keep = tiled(jnp.where(heads_agree, 1.0, 0.0), page_cols * pages_per_step // HEAD_DIM)
    acc, carry = attend([pages(k_refs, s) for s in seqs], [pages(v_refs, s) for s in seqs],
                        [acc_ref[s] for s in seqs], [carry_ref[s] for s in seqs], keep=keep)
    for s in seqs:
        acc_ref[s] = acc[s]
        carry_ref[s] = carry[s]

    @pl.when(p == pl.num_programs(1) - 1)
    def _():
        for s in seqs:
            o_ref[s] = _rms_normalize(acc[s], g_ref[...])


def _sb_decode(q_rows, bias_rows, k_new, v_new, cache_k, cache_v, page_table, g_rows, layer, t_new):
    n_seq, rows, _ = q_rows.shape
    n_pages = page_table.shape[1]
    page = cache_k.shape[2]
    assert cache_k.shape[3:] == (N_HEADS, HEAD_DIM) and rows <= HEAD_DIM
    pages_per_step = next(n for n in (4, 2, 1) if n_pages % n == 0)
    seqs_per_step = next(n for n in (2, 1) if n_seq % n == 0)
    row_spec = pl.BlockSpec((seqs_per_step, rows, HEAD_DIM), lambda sb, p, pt: (sb, 0, 0))
    const_spec = pl.BlockSpec((rows, HEAD_DIM), lambda sb, p, pt: (0, 0))

    def page_spec(s, i):
        def index_map(sb, p, pt):
            seq = sb * seqs_per_step + s
            return (layer, pt[seq * n_pages + n_pages - 1 - (p * pages_per_step + i)], 0, 0, 0)
        return pl.BlockSpec((1, 1, page, N_HEADS, HEAD_DIM), index_map)

    page_specs = [page_spec(s, i) for s in range(seqs_per_step) for i in range(pages_per_step)]
    scratch = pltpu.VMEM((seqs_per_step, rows, HEAD_DIM), F32)
    return pl.pallas_call(
        functools.partial(_sb_decode_kernel, t_new=t_new, pages_per_step=pages_per_step,
                          seqs_per_step=seqs_per_step),
        grid_spec=pltpu.PrefetchScalarGridSpec(
            num_scalar_prefetch=1,
            grid=(n_seq // seqs_per_step, n_pages // pages_per_step),
            in_specs=[row_spec, const_spec, row_spec, row_spec] + page_specs + page_specs + [const_spec],
            out_specs=row_spec,
            scratch_shapes=[scratch, scratch],
        ),
        out_shape=jax.ShapeDtypeStruct((n_seq, rows, HEAD_DIM), F32),
        compiler_params=_params("parallel", "arbitrary"),
        name="sb_decode",
    )(page_table.reshape(-1), q_rows, bias_rows, k_new, v_new,
      *([cache_k] * len(page_specs)), *([cache_v] * len(page_specs)), g_rows)


def _lower_bound(logits, layer):
    e = jnp.exp(logits - jnp.max(logits, axis=0, keepdims=True))
    return jnp.sum(e[:layer + 1], axis=0, keepdims=True) / jnp.sum(e, axis=0, keepdims=True)


def _hgrn_gates(hq, hf, lb, log=jnp.log):
    f = lb + (1.0 - lb) * _sigmoid(hf)
    return _silu(hq), 1.0 - f, log(f)


def _block_tril(n, blk):
    r = lax.broadcasted_iota(jnp.int32, (n, n), 0)
    c = lax.broadcasted_iota(jnp.int32, (n, n), 1)
    return jnp.where((c <= r) & (r // blk == c // blk), 1.0, 0.0).astype(BF16)


HG_HEADS_PER_STEP = 2


def _hgrn_prompt_kernel(hq_ref, hf_ref, hi_ref, hg_ref, lbl_ref, g_ref, o_ref, s_ref, st_ref,
                        *, tc, layer):
    t = pl.program_id(2)
    n_chunks = tc // HG_CHUNK
    n_sub = HG_CHUNK // HG_SUB
    heads = [slice(n * HEAD_DIM, (n + 1) * HEAD_DIM) for n in range(HG_HEADS_PER_STEP)]
    every = range(len(heads))

    @pl.when(t == 0)
    def _():
        st_ref[...] = jnp.zeros_like(st_ref)

    tril_chunk = _block_tril(tc, HG_CHUNK)
    tril_sub = _block_tril(tc, HG_SUB)
    rr =lax.broadcasted_iota(jnp.int32, (2 * HEAD_DIM, 2 * HEAD_DIM), 0) // HEAD_DIM
    cc = lax.broadcasted_iota(jnp.int32, (2 * HEAD_DIM, 2 * HEAD_DIM), 1) // HEAD_DIM
    pair_ones = jnp.where(rr == cc, 1.0, 0.0).astype(BF16)
    sub_shape = (tc // HG_SUB, HG_SUB, HEAD_DIM)
    chunk_shape = (n_chunks, HG_CHUNK, HEAD_DIM)
    r_sub = lax.broadcasted_iota(jnp.int32, sub_shape, 1)
    r_chunk = lax.broadcasted_iota(jnp.int32, chunk_shape, 1)

    q, kk, g, v = [], [], [], []
    for sl in heads:
        lb = _lower_bound(lbl_ref[:, sl], layer)
        qh, kh, gh = _hgrn_gates(hq_ref[:, sl].astype(F32), hf_ref[:, sl], lb, jnp.log2)
        q.append(qh)
        kk.append(kh)
        g.append(gh)
        v.append(hi_ref[:, sl].astype(F32))
    v_bf = [x.astype(BF16) for x in v]

    b, c = [], []
    for n in every:
        g_hi, g_lo = _split_hi_lo(g[n])
        b.append(jnp.dot(tril_chunk, g_hi, preferred_element_type=F32)
                 + jnp.dot(tril_chunk, g_lo, preferred_element_type=F32))
        c.append(jnp.dot(tril_sub, g_hi, preferred_element_type=F32)
                 + jnp.dot(tril_sub, g_lo, preferred_element_type=F32))

    scores, shifted_v = [], []
    for n in every:
        q3, k3, c3, v3 = (a.reshape(sub_shape) for a in (q[n], kk[n], c[n], v[n]))
        prods, rolled = [q3 * k3], [v3]
        for delta in range(1, HG_SUB):
            ks = pltpu.roll(k3, delta, axis=1)
            cs = pltpu.roll(c3, delta, axis=1)
            prods.append(jnp.where(r_sub >= delta, q3 * ks * jnp.exp2(c3 - cs), 0.0))
            rolled.append(pltpu.roll(v3, delta, axis=1))
        prods = [x.reshape(tc, HEAD_DIM).astype(BF16) for x in prods]
        paired = [jnp.concatenate(prods[d:d + 2], axis=-1) for d in range(0, HG_SUB, 2)]
        scores.append(jnp.dot(jnp.concatenate(paired, axis=0), pair_ones,
                              preferred_element_type=F32))
        shifted_v.append(rolled)
    o = []
    for n in every:
        acc = jnp.zeros((tc, HEAD_DIM), F32)
        for delta in range(HG_SUB):
            rows = slice((delta // 2) * tc, (delta // 2 + 1) * tc)
            cols = slice((delta % 2) * HEAD_DIM, (delta % 2 + 1) * HEAD_DIM)
            acc = acc + scores[n][rows, cols] * shifted_v[n][delta].reshape(tc, HEAD_DIM)
        o.append(acc)

    for n in every:
        b3 = b[n].reshape(chunk_shape)
        kk3 = kk[n].reshape(chunk_shape)
        q_sub = (q[n] * jnp.exp2(c[n])).reshape(chunk_shape)
        q_cat, k_cat = [], []
        for i in range(1, n_sub):
            lo_row = i * HG_SUB
            b_start = b3[:, lo_row - 1:lo_row, :]
            k_cat.append(jnp.where(r_chunk < lo_row,
                                   kk3 * jnp.exp2(jnp.minimum(b_start - b3, 0.0)), 0.0).astype(BF16))
            q_cat.append(jnp.where((r_chunk >= lo_row) & (r_chunk < lo_row + HG_SUB),
                                   q_sub, 0.0).astype(BF16))
        a_off = jnp.einsum('ctk,csk->cts', jnp.concatenate(q_cat, axis=-1),
                           jnp.concatenate(k_cat, axis=-1), preferred_element_type=F32)
        o_off = jnp.einsum('cts,csv->ctv', a_off.astype(BF16), v_bf[n].reshape(chunk_shape),
                           preferred_element_type=F32)
        o[n] = o[n] + o_off.reshape(tc, HEAD_DIM)

    q_in = [(q[n] * jnp.exp2(b[n])).astype(BF16) for n in every]
    st = [st_ref[n] for n in every]
    o_inter = [[] for _ in every]
    for ci in range(n_chunks):
        rows = slice(ci * HG_CHUNK, (ci + 1) * HG_CHUNK)
        for n in every:
            b_c = b[n][rows]
            b_last = b_c[HG_CHUNK - 1:HG_CHUNK]
            o_inter[n].append(_dot_nt(q_in[n][rows], st[n].astype(BF16)))
            k_out = (kk[n][rows] * jnp.exp2(b_last - b_c)).astype(BF16)
            st[n] = st[n] * jnp.exp2(b_last) + _dot_tn(v_bf[n][rows], k_out)

    for n, sl in enumerate(heads):
        st_ref[n] = st[n]
        total = o[n] + jnp.concatenate(o_inter[n], axis=0)
        gate = _silu(hg_ref[:, sl].astype(F32))
        o_ref[:, sl] = (_rms_normalize(total, g_ref[:, sl]) * gate).astype(o_ref.dtype)

    @pl.when(t == pl.num_programs(2) - 1)
    def _():
        for n in every:
            s_ref[0, n] = st[n].T


def _hgrn_prompt(narrow, hf, lb_logits, g, batch, seq, tc, layer):
    nt = seq // tc
    n_layers = lb_logits.shape[0]
    width = HG_HEADS_PER_STEP * HEAD_DIM
    assert N_HEADS % HG_HEADS_PER_STEP == 0
    tok = lambda b, h, t: (b * nt + t, h)
    return pl.pallas_call(
        functools.partial(_hgrn_prompt_kernel, tc=tc, layer=layer),
        grid=(batch, N_HEADS // HG_HEADS_PER_STEP, nt),
        in_specs=[
            _narrow_spec(tc, width, 3, tok),
            pl.BlockSpec((tc, width), tok),
            _narrow_spec(tc, width, 5, tok),
            _narrow_spec(tc, width, 6, tok),
            pl.BlockSpec((n_layers, width), lambda b, h, t: (0, h)),
            pl.BlockSpec((1, width), lambda b, h, t: (0, h)),
        ],
        out_specs=[
            pl.BlockSpec((tc, width), tok),
            pl.BlockSpec((1, HG_HEADS_PER_STEP, HEAD_DIM, HEAD_DIM), lambda b, h, t: (b, h, 0, 0)),
        ],
        out_shape=[
            jax.ShapeDtypeStruct((batch * seq, GROUP_WIDTH), BF16),
            jax.ShapeDtypeStruct((batch, N_HEADS, HEAD_DIM, HEAD_DIM), F32),
        ],
        scratch_shapes=[pltpu.VMEM((HG_HEADS_PER_STEP, HEAD_DIM, HEAD_DIM), F32)],
        compiler_params=_params("parallel", "parallel", "arbitrary"),
        name="hgrn_prompt",
    )(narrow, hf, narrow, narrow, lb_logits, g)


def _hgrn_decode_kernel(hq_ref, hf_ref, hi_ref, hg_ref, s_ref, lbl_ref, g_ref, o_ref, so_ref,
                        *, t_new, layer, seqs_per_step):
    heads = [slice(h * HEAD_DIM, (h + 1) * HEAD_DIM) for h in range(N_HEADS)]
    pairs = [(t, u) for t in range(t_new) for u in range(t + 1)]
    pair_rows = -(-len(pairs) // 16) * 16
    ones = jnp.ones((HEAD_DIM, HEAD_DIM), BF16)
    lb = _lower_bound(lbl_ref[...], layer)
    for s in range(seqs_per_step):
        q, kk, g = _hgrn_gates(hq_ref[s].astype(F32), hf_ref[s], lb, jnp.log2)
        v = hi_ref[s].astype(F32)
        b_rows = [g[0:1]]
        for t in range(1, t_new):
            b_rows.append(b_rows[-1] + g[t:t + 1])
        b_last = b_rows[-1]
        b = jnp.concatenate(b_rows, axis=0)

        prods = [q[t:t + 1] * kk[u:u + 1] * jnp.exp2(b_rows[t] - b_rows[u]) for t, u in pairs]
        prods.append(jnp.zeros((pair_rows - len(pairs), GROUP_WIDTH), F32))
        prods = jnp.concatenate(prods, axis=0)
        a = jnp.dot(jnp.concatenate([prods[:, sl] for sl in heads], axis=0).astype(BF16), ones,
                    preferred_element_type=F32)

        q_in = (q * jnp.exp2(b)).astype(BF16)
        pad = jnp.zeros((HG_SUB - t_new, GROUP_WIDTH), F32)
        k_out = jnp.concatenate([kk * jnp.exp2(b_last - b), pad], axis=0).astype(BF16)
        v_pad = jnp.concatenate([v, pad], axis=0).astype(BF16)
        decay_t = jnp.concatenate([jnp.exp2(b_last[:, sl]) for sl in heads], axis=0).T
        gate = _silu(hg_ref[s].astype(F32))
        for h, sl in enumerate(heads):
            state = s_ref[0, s, h]
            o = jnp.dot(q_in[:, sl], state.astype(BF16), preferred_element_type=F32)
            rows = []
            for t in range(t_new):
                row = o[t:t + 1]
                for u in range(t + 1):
                    i = h * pair_rows + pairs.index((t, u))
                    row = row + a[i:i + 1] * v[u:u + 1, sl]
                rows.append(row)
            o = jnp.concatenate(rows, axis=0)
            o_ref[s, :, sl] = _rms_normalize(o, g_ref[:, sl]) * gate[:, sl]
            so_ref[0, s, h] = state * decay_t[:, h:h + 1] + _dot_tn(k_out[:, sl], v_pad[:, sl])


def _hgrn_decode(hq, hf, hi, hg, state, lb_logits, g, layer):
    n_seq, t_new, _ = hq.shape
    assert t_new <= HG_SUB
    n_layers = lb_logits.shape[0]
    seqs_per_step = next(n for n in (2, 1) if n_seq % n == 0)
    row_spec = pl.BlockSpec((seqs_per_step, t_new, GROUP_WIDTH), lambda s: (s, 0, 0))
    state_block = (1, seqs_per_step, N_HEADS, HEAD_DIM, HEAD_DIM)
    return pl.pallas_call(
        functools.partial(_hgrn_decode_kernel, t_new=t_new, layer=layer,
                          seqs_per_step=seqs_per_step),
        grid=(n_seq // seqs_per_step,),
        in_specs=[
            row_spec, row_spec, row_spec, row_spec,
            pl.BlockSpec(state_block, lambda s: (layer, s, 0, 0, 0)),
            pl.BlockSpec((n_layers, GROUP_WIDTH), lambda s: (0, 0)),
            pl.BlockSpec((1, GROUP_WIDTH), lambda s: (0, 0)),
        ],
        out_specs=[
            row_spec,
            pl.BlockSpec(state_block, lambda s: (0, s, 0, 0, 0)),
        ],
        out_shape=[
            jax.ShapeDtypeStruct((n_seq, t_new, GROUP_WIDTH), F32),
            jax.ShapeDtypeStruct((1, n_seq, N_HEADS, HEAD_DIM, HEAD_DIM), F32),
        ],
        compiler_params=_params("parallel"),
        name="hgrn_decode",
    )(hq, hf, hi, hg, state, lb_logits, g)


def _outproj_kernel(sb_ref, hg_ref, w_ref, x_ref, o_ref):
    half = sb_ref.shape[1]
    acc = jnp.dot(sb_ref[...].astype(BF16), w_ref[:half, :], preferred_element_type=F32)
    acc = acc + jnp.dot(hg_ref[...].astype(BF16), w_ref[half:, :], preferred_element_type=F32)
    o_ref[...] = x_ref[...] + acc


def _outproj(sb, hg, w_bf16, x, tm, tn):
    m, d = x.shape
    width = sb.shape[1]
    return pl.pallas_call(
        _outproj_kernel,
        grid=(m // tm, d // tn),
        in_specs=[
            pl.BlockSpec((tm, width), lambda i, j: (i, 0)),
            pl.BlockSpec((tm, width), lambda i, j: (i, 0)),
            pl.BlockSpec((2 * width, tn), lambda i, j: (0, j)),
            pl.BlockSpec((tm, tn), lambda i, j: (i, j)),
        ],
        out_specs=pl.BlockSpec((tm, tn), lambda i, j: (i, j)),
        out_shape=jax.ShapeDtypeStruct((m, d), F32),
        compiler_params=_params("parallel", "arbitrary"),
        name="outproj",
    )(sb, hg, w_bf16, x)


def _mlp_kernel(h_ref, g2_ref, wu_ref, wd_ref, gf_ref, o_ref, hn_ref, acc_ref, *, final_norm):
    j = pl.program_id(1)

    @pl.when(j == 0)
    def _():
        hn_ref[...] = _rms_normalize(h_ref[...], g2_ref[...]).astype(BF16)
        acc_ref[...] = jnp.zeros_like(acc_ref)

    u = jnp.dot(hn_ref[...], wu_ref[...], preferred_element_type=F32)
    a = jnp.square(jnp.maximum(u, 0.0)).astype(BF16)
    acc_ref[...] += jnp.dot(a, wd_ref[...], preferred_element_type=F32)

    @pl.when(j == pl.num_programs(1) - 1)
    def _():
        y = h_ref[...] + acc_ref[...]
        o_ref[...] = _rms_normalize(y, gf_ref[...]) if final_norm else y


def _mlp(h, g2, wu_bf16, wd_bf16, gf, tm, tf, final_norm):
    m, d = h.shape
    f = wu_bf16.shape[1]
    return pl.pallas_call(
        functools.partial(_mlp_kernel, final_norm=final_norm),
        grid=(m // tm, f // tf),
        in_specs=[
            pl.BlockSpec((tm, d), lambda i, j: (i, 0)),
            pl.BlockSpec((1, d), lambda i, j: (0, 0)),
            pl.BlockSpec((d, tf), lambda i, j: (0, j)),
            pl.BlockSpec((tf, d), lambda i, j: (j, 0)),
            pl.BlockSpec((1, d), lambda i, j: (0, 0)),
        ],
        out_specs=pl.BlockSpec((tm, d), lambda i, j: (i, 0)),
        out_shape=jax.ShapeDtypeStruct((m, d), F32),
        scratch_shapes=[pltpu.VMEM((tm, d), BF16), pltpu.VMEM((tm, d), F32)],
        compiler_params=_params("parallel", "arbitrary"),
        name="mlp",
    )(h, g2, wu_bf16, wd_bf16, gf)


def _row_tile(m, target):
    return target if m % target == 0 else m


def kernel(x_prompt, x_sample, cache_k, cache_v, state_hgrn, page_table, norm1_g, w_in, sb_bias,
           sb_norm_g, hg_norm_g, hg_lb_logits, w_out, norm2_g, w_up, w_down, final_norm_g):
    batch, seq, d = x_prompt.shape
    n_seq, t_new, _ = x_sample.shape
    depth = w_in.shape[0]
    hp = x_prompt.reshape(batch * seq, d)
    hs = x_sample.reshape(n_seq * t_new, d)
    tm_p = _row_tile(batch * seq, 512)
    tm_s = _row_tile(n_seq * t_new, 512)
    tq = _row_tile(seq, 512)
    tk_sb = _row_tile(tq, 256)
    tc = _row_tile(seq, 256)
    gf = final_norm_g.reshape(1, d)

    outs = {k: [] for k in ("kp", "vp", "sp", "ks", "vs", "ss")}
    for l in range(depth):
        g1 = norm1_g[l].reshape(1, d)
        g2 = norm2_g[l].reshape(1, d)
        sbg = sb_norm_g[l].reshape(1, GROUP_WIDTH)
        hgg = hg_norm_g[l].reshape(1, GROUP_WIDTH)
        w_in_l = w_in[l].astype(BF16)
        w_out_l = w_out[l].astype(BF16)
        w_up_l = w_up[l].astype(BF16)
        w_down_l = w_down[l].astype(BF16)
        last = l == depth - 1

        sk, sv, hf, narrow = _inproj(hp, g1, w_in_l, tm_p)
        sb = _sb_prompt(narrow, sk, sv, sb_bias[l], sbg, batch, seq, tq, tk_sb)
        hgo, s_p = _hgrn_prompt(narrow, hf, hg_lb_logits, hgg, batch, seq, tc, l)
        hmid = _outproj(sb, hgo, w_out_l, hp, tm_p, 1024)
        hp = _mlp(hmid, g2, w_up_l, w_down_l, gf, tm_p, 1024, last)
        outs["kp"].append(sk.reshape(batch, seq, N_HEADS, HEAD_DIM))
        outs["vp"].append(sv.reshape(batch, seq, N_HEADS, HEAD_DIM))
        outs["sp"].append(s_p)

        *wide_s, narrow_s = _inproj(hs, g1, w_in_l, tm_s)
        tk, tv, uf = (a.reshape(n_seq, t_new, GROUP_WIDTH) for a in wide_s)
        tq_, uq, ui, ug = (narrow_s[n].reshape(n_seq, t_new, GROUP_WIDTH) for n in range(4))
        rows = N_HEADS * t_new
        sbg_rows = jnp.repeat(sbg.reshape(N_HEADS, HEAD_DIM), t_new, axis=0)
        bias_rows = jnp.broadcast_to(jnp.repeat(sb_bias[l], t_new)[:, None], (rows, HEAD_DIM))
        q_rows = tq_.reshape(n_seq, t_new, N_HEADS, HEAD_DIM).transpose(0, 2, 1, 3)
        sb2 = _sb_decode(q_rows.reshape(n_seq, rows, HEAD_DIM), bias_rows,
                         tk.reshape(n_seq, rows, HEAD_DIM), tv.reshape(n_seq, rows, HEAD_DIM),
                         cache_k, cache_v, page_table, sbg_rows, l, t_new)
        sb2 = sb2.reshape(n_seq, N_HEADS, t_new, HEAD_DIM).transpose(0, 2, 1, 3)
        sb2 = sb2.reshape(n_seq * t_new, GROUP_WIDTH)
        hgo2, s_s = _hgrn_decode(uq, uf, ui, ug, state_hgrn, hg_lb_logits, hgg, l)
        hmid2 = _outproj(sb2, hgo2.reshape(n_seq * t_new, GROUP_WIDTH), w_out_l, hs, tm_s, 1024)
        hs = _mlp(hmid2, g2, w_up_l, w_down_l, gf, tm_s, 1024, last)
        outs["ks"].append(tk.reshape(n_seq, t_new, N_HEADS, HEAD_DIM))
        outs["vs"].append(tv.reshape(n_seq, t_new, N_HEADS, HEAD_DIM))
        outs["ss"].append(s_s[0])

    y_prompt = hp.reshape(batch, seq, d)
    y_sample = hs.reshape(n_seq, t_new, d)
    return (y_prompt, y_sample, jnp.stack(outs["kp"]), jnp.stack(outs["vp"]), jnp.stack(outs["sp"]),
            jnp.stack(outs["ks"]), jnp.stack(outs["vs"]), jnp.stack(outs["ss"]))
```

```python
import functools

import jax
import jax.numpy as jnp
from jax import lax
from jax.experimental import pallas as pl
from jax.experimental.pallas import tpu as pltpu

F32 = jnp.float32
BF16 = jnp.bfloat16

RMS_EPS = 1e-6
HEAD_DIM = 128
N_HEADS = 8
GROUP_WIDTH = N_HEADS * HEAD_DIM
HG_CHUNK = 32
HG_SUB = 8
VMEM_LIMIT = 56 * 1024 * 1024


def _params(*sem):
    return pltpu.CompilerParams(dimension_semantics=sem, vmem_limit_bytes=VMEM_LIMIT)


def _rms_normalize(x, g):
    return x * lax.rsqrt(jnp.mean(x * x, axis=-1, keepdims=True) + RMS_EPS) * g


def _split_hi_lo(x):
    hi = x.astype(BF16)
    lo = (x - hi.astype(F32)).astype(BF16)
    return hi, lo


def _dot_nt(a, b):
    return lax.dot_general(a, b, (((1,), (1,)), ((), ())), preferred_element_type=F32)


def _dot_tn(a, b):
    return lax.dot_general(a, b, (((0,), (0,)), ((), ())), preferred_element_type=F32)


def _sigmoid(x):
    return 1.0 / (1.0 + jnp.exp(-x))


def _silu(x):
    return x * _sigmoid(x)


INPROJ_F32_GROUPS = (1, 2, 4)
INPROJ_BF16_GROUPS = (0, 3, 5, 6)


def _inproj_kernel(order_ref, x_ref, g_ref, w_ref, *refs, parts):
    del order_ref
    f32_refs, narrow_ref, xn_ref = refs[:-2], refs[-2], refs[-1]
    j = pl.program_id(1)

    @pl.when(j == 0)
    def _():
        xn_ref[...] = _rms_normalize(x_ref[...], g_ref[...]).astype(BF16)

    for n, o_ref in enumerate(f32_refs):
        @pl.when(j // parts == n)
        def _(o_ref=o_ref):
            o_ref[...] = jnp.dot(xn_ref[...], w_ref[...], preferred_element_type=F32)

    @pl.when(j >= len(f32_refs) * parts)
    def _():
        narrow_ref[...] = jnp.dot(xn_ref[...], w_ref[...],
                                  preferred_element_type=F32).astype(BF16)


def _inproj(x, g, w_bf16, tm, tn):
    m, d = x.shape
    n_wide, n_narrow = len(INPROJ_F32_GROUPS), len(INPROJ_BF16_GROUPS)
    assert w_bf16.shape[1] == (n_wide + n_narrow) * GROUP_WIDTH and m % tm == 0
    parts = GROUP_WIDTH // tn
    order = jnp.array(INPROJ_F32_GROUPS + INPROJ_BF16_GROUPS, jnp.int32)

    def f32_spec(n):
        return pl.BlockSpec((tm, tn), lambda i, j, order: (i, jnp.clip(j - n * parts, 0, parts - 1)))

    def narrow_map(i, j, order):
        jn = jnp.maximum(j - n_wide * parts, 0)
        return (jn // parts, i, jn % parts)

    return pl.pallas_call(
        functools.partial(_inproj_kernel, parts=parts),
        grid_spec=pltpu.PrefetchScalarGridSpec(
            num_scalar_prefetch=1,
            grid=(m // tm, (n_wide + n_narrow) * parts),
            in_specs=[
                pl.BlockSpec((tm, d), lambda i, j, order: (i, 0)),
                pl.BlockSpec((1, d), lambda i, j, order: (0, 0)),
                pl.BlockSpec((d, tn), lambda i, j, order: (0, order[j // parts] * parts + j % parts)),
            ],
            out_specs=[f32_spec(n) for n in range(n_wide)]
            + [pl.BlockSpec((None, tm, tn), narrow_map)],
            scratch_shapes=[pltpu.VMEM((tm, d), BF16)],
        ),
        out_shape=[jax.ShapeDtypeStruct((m, GROUP_WIDTH), F32)] * n_wide
        + [jax.ShapeDtypeStruct((n_narrow, m, GROUP_WIDTH), BF16)],
        compiler_params=_params("parallel", "arbitrary"),
        name="inproj",
    )(order, x, g, w_bf16)


SB_SUM_PARTS = 1


def _suffix_matrix():
    shape = (SB_SUM_PARTS * HEAD_DIM, 2 * HEAD_DIM)
    r = lax.broadcasted_iota(jnp.int32, shape, 0) % HEAD_DIM
    c = lax.broadcasted_iota(jnp.int32, shape, 1)
    return jnp.where((c >= HEAD_DIM) | (r > c), 1.0, 0.0).astype(BF16)


LOG2_E = 1.4426950408889634


def _log_break(z2):
    l = jnp.log2(1.0 + jnp.exp2(-jnp.abs(z2)))
    log_beta = jnp.minimum(z2, 0.0) - l
    return log_beta, log_beta - z2


def _half_sums(log_keep, suffix_mat):
    parts = _split_hi_lo(log_keep) if suffix_mat.shape[0] == 2 * HEAD_DIM else (log_keep.astype(BF16),)
    rows, n = log_keep.shape[0], log_keep.shape[1] // HEAD_DIM
    stacked = []
    for s in range(n):
        sl = slice(s * HEAD_DIM, (s + 1) * HEAD_DIM)
        stacked.append(jnp.concatenate([p[:, sl] for p in parts], axis=-1))
    c = jnp.dot(jnp.concatenate(stacked, axis=0) if n > 1 else stacked[0], suffix_mat,
                preferred_element_type=F32)
    return [(c[s * rows:(s + 1) * rows, :HEAD_DIM], c[s * rows:(s + 1) * rows, HEAD_DIM:])
            for s in range(n)]


def _sb_sums(z2, suffix_mat, visible=None, keep=None):
    log_beta, log_keep = _log_break(z2)
    if keep is not None:
        log_keep = log_keep * keep
    if visible is not None:
        log_keep = jnp.where(visible, log_keep, 0.0)
    return log_beta, _half_sums(log_keep, suffix_mat)


def _sb_weights(log_beta, halves, carry, visible=None, keep=None):
    after = []
    for suffix, total in reversed(halves):
        after.append(suffix + carry)
        carry = carry + total
    after = jnp.concatenate(after[::-1], axis=-1) if len(after) > 1 else after[0]
    w = jnp.exp2(log_beta + after)
    if keep is not None:
        w = w * keep
    if visible is not None:
        w = jnp.where(visible, w, 0.0)
    return w, carry


def _sb_block(z2, carry, suffix_mat, **mask):
    log_beta, halves = _sb_sums(z2, suffix_mat, **mask)
    return _sb_weights(log_beta, halves, carry, **mask)


SB_HEADS_PER_STEP = 4


def _sb_prompt_kernel(bias_ref, q_ref, k_ref, v_ref, g_ref, o_ref, *, tq, tk):
    hb = pl.program_id(1)
    i = pl.program_id(2)
    scale2 = HEAD_DIM ** -0.5 * LOG2_E
    suffix_mat = _suffix_matrix()
    n_diag = tq // tk
    heads = [slice(n * HEAD_DIM, (n + 1) * HEAD_DIM) for n in range(SB_HEADS_PER_STEP)]
    qs = [q_ref[:, sl] for sl in heads]
    bias2 = [bias_ref[hb * SB_HEADS_PER_STEP + n] * LOG2_E for n in range(SB_HEADS_PER_STEP)]

    def block(j, state, visible):
        start = pl.multiple_of(j * tk, tk)
        n_heads = len(heads)
        z, sums, new = [None] * n_heads, [None] * n_heads, [None] * n_heads
        for step in range(n_heads + 2):
            n = step
            if n < n_heads:
                kb = (k_ref[pl.ds(start, tk), heads[n]] * scale2).astype(BF16)
                z[n] = _dot_nt(qs[n], kb) + bias2[n]
            n = step - 1
            if 0 <= n < n_heads:
                sums[n] = _sb_sums(z[n], suffix_mat, visible)
            n = step - 2
            if 0 <= n < n_heads:
                acc, carry = state[n]
                w, carry = _sb_weights(*sums[n], carry, visible)
                vb = v_ref[pl.ds(start, tk), heads[n]].astype(BF16)
                new[n] = (acc + jnp.dot(w.astype(BF16), vb, preferred_element_type=F32), carry)
        return tuple(new)

    row = lax.broadcasted_iota(jnp.int32, (tq, tk), 0)
    col = lax.broadcasted_iota(jnp.int32, (tq, tk), 1)
    zeros = jnp.zeros((tq, HEAD_DIM), F32)
    state = tuple((zeros, zeros) for _ in heads)
    for d in reversed(range(n_diag)):
        state = block(i * n_diag + d, state, col + d * tk < row)

    n_before = i * n_diag
    state = lax.fori_loop(0, n_before, lambda jj, st: block(n_before - 1 - jj, st, None), state)
    for n, sl in enumerate(heads):
        o_ref[:, sl] = _rms_normalize(state[n][0], g_ref[:, sl]).astype(o_ref.dtype)


def _narrow_spec(rows, width, group, index_map):
    slot = INPROJ_BF16_GROUPS.index(group)
    return pl.BlockSpec((None, rows, width), lambda *idx: (slot,) + tuple(index_map(*idx)))


def _sb_prompt(narrow, k, v, bias, g, batch, seq, tq, tk):
    nq = seq // tq
    width = SB_HEADS_PER_STEP * HEAD_DIM
    assert tq % tk == 0 and tk % HEAD_DIM == 0 and N_HEADS % SB_HEADS_PER_STEP == 0
    return pl.pallas_call(
        functools.partial(_sb_prompt_kernel, tq=tq, tk=tk),
        grid=(batch, N_HEADS // SB_HEADS_PER_STEP, nq),
        in_specs=[
            pl.BlockSpec(memory_space=pltpu.SMEM),
            _narrow_spec(tq, width, 0, lambda b, h, i: (b * nq + i, h)),
            pl.BlockSpec((seq, width), lambda b, h, i: (b, h)),
            pl.BlockSpec((seq, width), lambda b, h, i: (b, h)),
            pl.BlockSpec((1, width), lambda b, h, i: (0, h)),
        ],
        out_specs=pl.BlockSpec((tq, width), lambda b, h, i: (b * nq + i, h)),
        out_shape=jax.ShapeDtypeStruct((batch * seq, GROUP_WIDTH), BF16),
        compiler_params=_params("parallel", "parallel", "arbitrary"),
        name="sb_prompt",
    )(bias, narrow, k, v, g)


def _sb_decode_kernel(pt_ref, q_ref, bias_ref, kn_ref, vn_ref, *refs, t_new, pages_per_step,
                      seqs_per_step):
    del pt_ref
    n_page_refs = seqs_per_step * pages_per_step
    k_refs, v_refs = refs[:n_page_refs], refs[n_page_refs:2 * n_page_refs]
    g_ref, o_ref, acc_ref, carry_ref = refs[2 * n_page_refs:]
    p = pl.program_id(1)
    rows = N_HEADS * t_new
    page_cols = k_refs[0].shape[2] * N_HEADS
    scale2 = HEAD_DIM ** -0.5 * LOG2_E
    suffix_mat = _suffix_matrix()
    seqs = range(seqs_per_step)

    def tiled(x, n):
        return jnp.concatenate([x] * n, axis=-1) if n > 1 else x

    def attend(keys, values, acc, carry, **mask):
        n = keys[0].shape[0] // HEAD_DIM
        bias2 = tiled(bias_ref[...] * LOG2_E, n)
        z2 = [_dot_nt(q_ref[s], keys[s].astype(BF16)) * scale2 + bias2 for s in seqs]
        sums = [_sb_sums(z2[s], suffix_mat, **mask) for s in seqs]
        out = []
        for s in seqs:
            w, new_carry = _sb_weights(*sums[s], carry[s], **mask)
            out.append((acc[s] + jnp.dot(w.astype(BF16), values[s].astype(BF16),
                                         preferred_element_type=F32), new_carry))
        return [o[0] for o in out], [o[1] for o in out]

    r = lax.broadcasted_iota(jnp.int32, (rows, HEAD_DIM), 0)
    c = lax.broadcasted_iota(jnp.int32, (rows, HEAD_DIM), 1)
    heads_agree = (c % N_HEADS) == (r // t_new)

    @pl.when(p == 0)
    def _():
        pad = jnp.zeros((HEAD_DIM - rows, HEAD_DIM), F32)
        kn = [jnp.concatenate([kn_ref[s], pad], axis=0) for s in seqs]
        vn = [jnp.concatenate([vn_ref[s], pad], axis=0) for s in seqs]
        zeros = [jnp.zeros((rows, HEAD_DIM), F32)] * seqs_per_step
        acc, carry = attend(kn, vn, zeros, zeros, visible=heads_agree & (c // N_HEADS < r % t_new))
        for s in seqs:
            acc_ref[s] = acc[s]
            carry_ref[s] = carry[s]

    def pages(page_refs, s):
        mine = page_refs[s * pages_per_step:(s + 1) * pages_per_step]
        return jnp.concatenate([ref[0, 0].reshape(page_cols, HEAD_DIM) for ref in mine[::-1]], axis=0)

    keep = tiled(jnp.where(heads_agree, 1.0, 0.0), page_cols * pages_per_step // HEAD_DIM)
    acc, carry = attend([pages(k_refs, s) for s in seqs], [pages(v_refs, s) for s in seqs],
                        [acc_ref[s] for s in seqs], [carry_ref[s] for s in seqs], keep=keep)
    for s in seqs:
        acc_ref[s] = acc[s]
        carry_ref[s] = carry[s]

    @pl.when(p == pl.num_programs(1) - 1)
    def _():
        for s in seqs:
            o_ref[s] = _rms_normalize(acc[s], g_ref[...])


def _sb_decode(q_rows, bias_rows, k_new, v_new, cache_k, cache_v, page_table, g_rows, layer, t_new):
    n_seq, rows, _ = q_rows.shape
    n_pages = page_table.shape[1]
    page = cache_k.shape[2]
    assert cache_k.shape[3:] == (N_HEADS, HEAD_DIM) and rows <= HEAD_DIM
    pages_per_step = next(n for n in (4, 2, 1) if n_pages % n == 0)
    seqs_per_step = next(n for n in (2, 1) if n_seq % n == 0)
    row_spec = pl.BlockSpec((seqs_per_step, rows, HEAD_DIM), lambda sb, p, pt: (sb, 0, 0))
    const_spec = pl.BlockSpec((rows, HEAD_DIM), lambda sb, p, pt: (0, 0))

    def page_spec(s, i):
        def index_map(sb, p, pt):
            seq = sb * seqs_per_step + s
            return (layer, pt[seq * n_pages + n_pages - 1 - (p * pages_per_step + i)], 0, 0, 0)
        return pl.BlockSpec((1, 1, page, N_HEADS, HEAD_DIM), index_map)

    page_specs = [page_spec(s, i) for s in range(seqs_per_step) for i in range(pages_per_step)]
    scratch = pltpu.VMEM((seqs_per_step, rows, HEAD_DIM), F32)
    return pl.pallas_call(
        functools.partial(_sb_decode_kernel, t_new=t_new, pages_per_step=pages_per_step,
                          seqs_per_step=seqs_per_step),
        grid_spec=pltpu.PrefetchScalarGridSpec(
            num_scalar_prefetch=1,
            grid=(n_seq // seqs_per_step, n_pages // pages_per_step),
            in_specs=[row_spec, const_spec, row_spec, row_spec] + page_specs + page_specs + [const_spec],
            out_specs=row_spec,
            scratch_shapes=[scratch, scratch],
        ),
        out_shape=jax.ShapeDtypeStruct((n_seq, rows, HEAD_DIM), F32),
        compiler_params=_params("parallel", "arbitrary"),
        name="sb_decode",
    )(page_table.reshape(-1), q_rows, bias_rows, k_new, v_new,
      *([cache_k] * len(page_specs)), *([cache_v] * len(page_specs)), g_rows)


def _lower_bound(logits, layer):
    e = jnp.exp(logits - jnp.max(logits, axis=0, keepdims=True))
    return jnp.sum(e[:layer + 1], axis=0, keepdims=True) / jnp.sum(e, axis=0, keepdims=True)


def _hgrn_gates(hq, hf, lb, log=jnp.log):
    f = lb + (1.0 - lb) * _sigmoid(hf)
    return _silu(hq), 1.0 - f, log(f)


def _block_tril(n, blk):
    r = lax.broadcasted_iota(jnp.int32, (n, n), 0)
    c = lax.broadcasted_iota(jnp.int32, (n, n), 1)
    return jnp.where((c <= r) & (r // blk == c // blk), 1.0, 0.0).astype(BF16)


HG_HEADS_PER_STEP = 4


def _hgrn_prompt_kernel(hq_ref, hf_ref, hi_ref, hg_ref, lbl_ref, g_ref, o_ref, s_ref, st_ref,
                        *, tc, layer):
    t = pl.program_id(2)
    n_chunks = tc // HG_CHUNK
    n_sub = HG_CHUNK // HG_SUB
    heads = [slice(n * HEAD_DIM, (n + 1) * HEAD_DIM) for n in range(HG_HEADS_PER_STEP)]
    every = range(len(heads))

    @pl.when(t == 0)
    def _():
        st_ref[...] = jnp.zeros_like(st_ref)

    tril_chunk = _block_tril(tc, HG_CHUNK)
    tril_sub = _block_tril(tc, HG_SUB)
    rr =lax.broadcasted_iota(jnp.int32, (2 * HEAD_DIM, 2 * HEAD_DIM), 0) // HEAD_DIM
    cc = lax.broadcasted_iota(jnp.int32, (2 * HEAD_DIM, 2 * HEAD_DIM), 1) // HEAD_DIM
    pair_ones = jnp.where(rr == cc, 1.0, 0.0).astype(BF16)
    sub_shape = (tc // HG_SUB, HG_SUB, HEAD_DIM)
    chunk_shape = (n_chunks, HG_CHUNK, HEAD_DIM)
    r_sub = lax.broadcasted_iota(jnp.int32, sub_shape, 1)
    r_chunk = lax.broadcasted_iota(jnp.int32, chunk_shape, 1)

    q, kk, g, v = [], [], [], []
    for sl in heads:
        lb = _lower_bound(lbl_ref[:, sl], layer)
        qh, kh, gh = _hgrn_gates(hq_ref[:, sl].astype(F32), hf_ref[:, sl], lb, jnp.log2)
        q.append(qh)
        kk.append(kh)
        g.append(gh)
        v.append(hi_ref[:, sl].astype(F32))
    v_bf = [x.astype(BF16) for x in v]

    b, c = [], []
    for n in every:
        g_hi, g_lo = _split_hi_lo(g[n])
        b.append(jnp.dot(tril_chunk, g_hi, preferred_element_type=F32)
                 + jnp.dot(tril_chunk, g_lo, preferred_element_type=F32))
        c.append(jnp.dot(tril_sub, g_hi, preferred_element_type=F32)
                 + jnp.dot(tril_sub, g_lo, preferred_element_type=F32))

    scores, shifted_v = [], []
    for n in every:
        q3, k3, c3, v3 = (a.reshape(sub_shape) for a in (q[n], kk[n], c[n], v[n]))
        prods, rolled = [q3 * k3], [v3]
        for delta in range(1, HG_SUB):
            ks = pltpu.roll(k3, delta, axis=1)
            cs = pltpu.roll(c3, delta, axis=1)
            prods.append(jnp.where(r_sub >= delta, q3 * ks * jnp.exp2(c3 - cs), 0.0))
            rolled.append(pltpu.roll(v3, delta, axis=1))
        prods = [x.reshape(tc, HEAD_DIM).astype(BF16) for x in prods]
        paired = [jnp.concatenate(prods[d:d + 2], axis=-1) for d in range(0, HG_SUB, 2)]
        scores.append(jnp.dot(jnp.concatenate(paired, axis=0), pair_ones,
                              preferred_element_type=F32))
        shifted_v.append(rolled)
    o = []
    for n in every:
        acc = jnp.zeros((tc, HEAD_DIM), F32)
        for delta in range(HG_SUB):
            rows = slice((delta // 2) * tc, (delta // 2 + 1) * tc)
            cols = slice((delta % 2) * HEAD_DIM, (delta % 2 + 1) * HEAD_DIM)
            acc = acc + scores[n][rows, cols] * shifted_v[n][delta].reshape(tc, HEAD_DIM)
        o.append(acc)

    for n in every:
        b3 = b[n].reshape(chunk_shape)
        kk3 = kk[n].reshape(chunk_shape)
        q_sub = (q[n] * jnp.exp2(c[n])).reshape(chunk_shape)
        q_cat, k_cat = [], []
        for i in range(1, n_sub):
            lo_row = i * HG_SUB
            b_start = b3[:, lo_row - 1:lo_row, :]
            k_cat.append(jnp.where(r_chunk < lo_row,
                                   kk3 * jnp.exp2(jnp.minimum(b_start - b3, 0.0)), 0.0).astype(BF16))
            q_cat.append(jnp.where((r_chunk >= lo_row) & (r_chunk < lo_row + HG_SUB),
                                   q_sub, 0.0).astype(BF16))
        a_off = jnp.einsum('ctk,csk->cts', jnp.concatenate(q_cat, axis=-1),
                           jnp.concatenate(k_cat, axis=-1), preferred_element_type=F32)
        o_off = jnp.einsum('cts,csv->ctv', a_off.astype(BF16), v_bf[n].reshape(chunk_shape),
                           preferred_element_type=F32)
        o[n] = o[n] + o_off.reshape(tc, HEAD_DIM)

    q_in = [(q[n] * jnp.exp2(b[n])).astype(BF16) for n in every]
    st = [st_ref[n] for n in every]
    o_inter = [[] for _ in every]
    for ci in range(n_chunks):
        rows = slice(ci * HG_CHUNK, (ci + 1) * HG_CHUNK)
        for n in every:
            b_c = b[n][rows]
            b_last = b_c[HG_CHUNK - 1:HG_CHUNK]
            o_inter[n].append(_dot_nt(q_in[n][rows], st[n].astype(BF16)))
            k_out = (kk[n][rows] * jnp.exp2(b_last - b_c)).astype(BF16)
            st[n] = st[n] * jnp.exp2(b_last) + _dot_tn(v_bf[n][rows], k_out)

    for n, sl in enumerate(heads):
        st_ref[n] = st[n]
        total = o[n] + jnp.concatenate(o_inter[n], axis=0)
        gate = _silu(hg_ref[:, sl].astype(F32))
        o_ref[:, sl] = (_rms_normalize(total, g_ref[:, sl]) * gate).astype(o_ref.dtype)

    @pl.when(t == pl.num_programs(2) - 1)
    def _():
        for n in every:
            s_ref[0, n] = st[n].T


def _hgrn_prompt(narrow, hf, lb_logits, g, batch, seq, tc, layer):
    nt = seq // tc
    n_layers = lb_logits.shape[0]
    width = HG_HEADS_PER_STEP * HEAD_DIM
    assert N_HEADS % HG_HEADS_PER_STEP == 0
    tok = lambda b, h, t: (b * nt + t, h)
    return pl.pallas_call(
        functools.partial(_hgrn_prompt_kernel, tc=tc, layer=layer),
        grid=(batch, N_HEADS // HG_HEADS_PER_STEP, nt),
        in_specs=[
            _narrow_spec(tc, width, 3, tok),
            pl.BlockSpec((tc, width), tok),
            _narrow_spec(tc, width, 5, tok),
            _narrow_spec(tc, width, 6, tok),
            pl.BlockSpec((n_layers, width), lambda b, h, t: (0, h)),
            pl.BlockSpec((1, width), lambda b, h, t: (0, h)),
        ],
        out_specs=[
            pl.BlockSpec((tc, width), tok),
            pl.BlockSpec((1, HG_HEADS_PER_STEP, HEAD_DIM, HEAD_DIM), lambda b, h, t: (b, h, 0, 0)),
        ],
        out_shape=[
            jax.ShapeDtypeStruct((batch * seq, GROUP_WIDTH), BF16),
            jax.ShapeDtypeStruct((batch, N_HEADS, HEAD_DIM, HEAD_DIM), F32),
        ],
        scratch_shapes=[pltpu.VMEM((HG_HEADS_PER_STEP, HEAD_DIM, HEAD_DIM), F32)],
        compiler_params=_params("parallel", "parallel", "arbitrary"),
        name="hgrn_prompt",
    )(narrow, hf, narrow, narrow, lb_logits, g)


def _hgrn_decode_kernel(hq_ref, hf_ref, hi_ref, hg_ref, s_ref, lbl_ref, g_ref, o_ref, so_ref,
                        *, t_new, layer, seqs_per_step):
    heads = [slice(h * HEAD_DIM, (h + 1) * HEAD_DIM) for h in range(N_HEADS)]
    pairs = [(t, u) for t in range(t_new) for u in range(t + 1)]
    pair_rows = -(-len(pairs) // 16) * 16
    ones = jnp.ones((HEAD_DIM, HEAD_DIM), BF16)
    lb = _lower_bound(lbl_ref[...], layer)
    for s in range(seqs_per_step):
        q, kk, g = _hgrn_gates(hq_ref[s].astype(F32), hf_ref[s], lb, jnp.log2)
        v = hi_ref[s].astype(F32)
        b_rows = [g[0:1]]
        for t in range(1, t_new):
            b_rows.append(b_rows[-1] + g[t:t + 1])
        b_last = b_rows[-1]
        b = jnp.concatenate(b_rows, axis=0)

        prods = [q[t:t + 1] * kk[u:u + 1] * jnp.exp2(b_rows[t] - b_rows[u]) for t, u in pairs]
        prods.append(jnp.zeros((pair_rows - len(pairs), GROUP_WIDTH), F32))
        prods = jnp.concatenate(prods, axis=0)
        a = jnp.dot(jnp.concatenate([prods[:, sl] for sl in heads], axis=0).astype(BF16), ones,
                    preferred_element_type=F32)

        q_in = (q * jnp.exp2(b)).astype(BF16)
        pad = jnp.zeros((HG_SUB - t_new, GROUP_WIDTH), F32)
        k_out = jnp.concatenate([kk * jnp.exp2(b_last - b), pad], axis=0).astype(BF16)
        v_pad = jnp.concatenate([v, pad], axis=0).astype(BF16)
        decay_t = jnp.concatenate([jnp.exp2(b_last[:, sl]) for sl in heads], axis=0).T
        gate = _silu(hg_ref[s].astype(F32))
        for h, sl in enumerate(heads):
            state = s_ref[0, s, h]
            o = jnp.dot(q_in[:, sl], state.astype(BF16), preferred_element_type=F32)
            rows = []
            for t in range(t_new):
                row = o[t:t + 1]
                for u in range(t + 1):
                    i = h * pair_rows + pairs.index((t, u))
                    row = row + a[i:i + 1] * v[u:u + 1, sl]
                rows.append(row)
            o = jnp.concatenate(rows, axis=0)
            o_ref[s, :, sl] = _rms_normalize(o, g_ref[:, sl]) * gate[:, sl]
            so_ref[0, s, h] = state * decay_t[:, h:h + 1] + _dot_tn(k_out[:, sl], v_pad[:, sl])


def _hgrn_decode(hq, hf, hi, hg, state, lb_logits, g, layer):
    n_seq, t_new, _ = hq.shape
    assert t_new <= HG_SUB
    n_layers = lb_logits.shape[0]
    seqs_per_step = next(n for n in (2, 1) if n_seq % n == 0)
    row_spec = pl.BlockSpec((seqs_per_step, t_new, GROUP_WIDTH), lambda s: (s, 0, 0))
    state_block = (1, seqs_per_step, N_HEADS, HEAD_DIM, HEAD_DIM)
    return pl.pallas_call(
        functools.partial(_hgrn_decode_kernel, t_new=t_new, layer=layer,
                          seqs_per_step=seqs_per_step),
        grid=(n_seq // seqs_per_step,),
        in_specs=[
            row_spec, row_spec, row_spec, row_spec,
            pl.BlockSpec(state_block, lambda s: (layer, s, 0, 0, 0)),
            pl.BlockSpec((n_layers, GROUP_WIDTH), lambda s: (0, 0)),
            pl.BlockSpec((1, GROUP_WIDTH), lambda s: (0, 0)),
        ],
        out_specs=[
            row_spec,
            pl.BlockSpec(state_block, lambda s: (0, s, 0, 0, 0)),
        ],
        out_shape=[
            jax.ShapeDtypeStruct((n_seq, t_new, GROUP_WIDTH), F32),
            jax.ShapeDtypeStruct((1, n_seq, N_HEADS, HEAD_DIM, HEAD_DIM), F32),
        ],
        compiler_params=_params("parallel"),
        name="hgrn_decode",
    )(hq, hf, hi, hg, state, lb_logits, g)


def _outproj_kernel(sb_ref, hg_ref, w_ref, x_ref, o_ref):
    half = sb_ref.shape[1]
    acc = jnp.dot(sb_ref[...].astype(BF16), w_ref[:half, :], preferred_element_type=F32)
    acc = acc + jnp.dot(hg_ref[...].astype(BF16), w_ref[half:, :], preferred_element_type=F32)
    o_ref[...] = x_ref[...] + acc


def _outproj(sb, hg, w_bf16, x, tm, tn):
    m, d = x.shape
    width = sb.shape[1]
    return pl.pallas_call(
        _outproj_kernel,
        grid=(m // tm, d // tn),
        in_specs=[
            pl.BlockSpec((tm, width), lambda i, j: (i, 0)),
            pl.BlockSpec((tm, width), lambda i, j: (i, 0)),
            pl.BlockSpec((2 * width, tn), lambda i, j: (0, j)),
            pl.BlockSpec((tm, tn), lambda i, j: (i, j)),
        ],
        out_specs=pl.BlockSpec((tm, tn), lambda i, j: (i, j)),
        out_shape=jax.ShapeDtypeStruct((m, d), F32),
        compiler_params=_params("parallel", "arbitrary"),
        name="outproj",
    )(sb, hg, w_bf16, x)


def _mlp_kernel(h_ref, g2_ref, wu_ref, wd_ref, gf_ref, o_ref, hn_ref, acc_ref, *, final_norm):
    j = pl.program_id(1)

    @pl.when(j == 0)
    def _():
        hn_ref[...] = _rms_normalize(h_ref[...], g2_ref[...]).astype(BF16)
        acc_ref[...] = jnp.zeros_like(acc_ref)

    u = jnp.dot(hn_ref[...], wu_ref[...], preferred_element_type=F32)
    a = jnp.square(jnp.maximum(u, 0.0)).astype(BF16)
    acc_ref[...] += jnp.dot(a, wd_ref[...], preferred_element_type=F32)

    @pl.when(j == pl.num_programs(1) - 1)
    def _():
        y = h_ref[...] + acc_ref[...]
        o_ref[...] = _rms_normalize(y, gf_ref[...]) if final_norm else y


def _mlp(h, g2, wu_bf16, wd_bf16, gf, tm, tf, final_norm):
    m, d = h.shape
    f = wu_bf16.shape[1]
    return pl.pallas_call(
        functools.partial(_mlp_kernel, final_norm=final_norm),
        grid=(m // tm, f // tf),
        in_specs=[
            pl.BlockSpec((tm, d), lambda i, j: (i, 0)),
            pl.BlockSpec((1, d), lambda i, j: (0, 0)),
            pl.BlockSpec((d, tf), lambda i, j: (0, j)),
            pl.BlockSpec((tf, d), lambda i, j: (j, 0)),
            pl.BlockSpec((1, d), lambda i, j: (0, 0)),
        ],
        out_specs=pl.BlockSpec((tm, d), lambda i, j: (i, 0)),
        out_shape=jax.ShapeDtypeStruct((m, d), F32),
        scratch_shapes=[pltpu.VMEM((tm, d), BF16), pltpu.VMEM((tm, d), F32)],
        compiler_params=_params("parallel", "arbitrary"),
        name="mlp",
    )(h, g2, wu_bf16, wd_bf16, gf)


def _row_tile(m, target):
    return target if m % target == 0 else m


def kernel(x_prompt, x_sample, cache_k, cache_v, state_hgrn, page_table, norm1_g, w_in, sb_bias,
           sb_norm_g, hg_norm_g, hg_lb_logits, w_out, norm2_g, w_up, w_down, final_norm_g):
    batch, seq, d = x_prompt.shape
    n_seq, t_new, _ = x_sample.shape
    depth = w_in.shape[0]
    hp = x_prompt.reshape(batch * seq, d)
    hs = x_sample.reshape(n_seq * t_new, d)
    tm_in_p = _row_tile(batch * seq, 1024)
    tm_out_p = _row_tile(batch * seq, 1024)
    tm_p = _row_tile(batch * seq, 512)
    tm_s = _row_tile(n_seq * t_new, 512)
    tn_in = 512
    tq = _row_tile(seq, 512)
    tk_sb = _row_tile(tq, 256)
    tc = _row_tile(seq, 256)
    gf = final_norm_g.reshape(1, d)

    outs = {k: [] for k in ("kp", "vp", "sp", "ks", "vs", "ss")}
    for l in range(depth):
        g1 = norm1_g[l].reshape(1, d)
        g2 = norm2_g[l].reshape(1, d)
        sbg = sb_norm_g[l].reshape(1, GROUP_WIDTH)
        hgg = hg_norm_g[l].reshape(1, GROUP_WIDTH)
        w_in_l = w_in[l].astype(BF16)
        w_out_l = w_out[l].astype(BF16)
        w_up_l = w_up[l].astype(BF16)
        w_down_l = w_down[l].astype(BF16)
        last = l == depth - 1

        sk, sv, hf, narrow = _inproj(hp, g1, w_in_l, tm_in_p, tn_in)
        sb = _sb_prompt(narrow, sk, sv, sb_bias[l], sbg, batch, seq, tq, tk_sb)
        hgo, s_p = _hgrn_prompt(narrow, hf, hg_lb_logits, hgg, batch, seq, tc, l)
        hmid = _outproj(sb, hgo, w_out_l, hp, tm_out_p, 1024)
        hp = _mlp(hmid, g2, w_up_l, w_down_l, gf, tm_p, 1024, last)
        outs["kp"].append(sk.reshape(batch, seq, N_HEADS, HEAD_DIM))
        outs["vp"].append(sv.reshape(batch, seq, N_HEADS, HEAD_DIM))
        outs["sp"].append(s_p)

        *wide_s, narrow_s = _inproj(hs, g1, w_in_l, tm_s, tn_in)
        tk, tv, uf = (a.reshape(n_seq, t_new, GROUP_WIDTH) for a in wide_s)
        tq_, uq, ui, ug = (narrow_s[n].reshape(n_seq, t_new, GROUP_WIDTH) for n in range(4))
        rows = N_HEADS * t_new
        sbg_rows = jnp.repeat(sbg.reshape(N_HEADS, HEAD_DIM), t_new, axis=0)
        bias_rows = jnp.broadcast_to(jnp.repeat(sb_bias[l], t_new)[:, None], (rows, HEAD_DIM))
        q_rows = tq_.reshape(n_seq, t_new, N_HEADS, HEAD_DIM).transpose(0, 2, 1, 3)
        sb2 = _sb_decode(q_rows.reshape(n_seq, rows, HEAD_DIM), bias_rows,
                         tk.reshape(n_seq, rows, HEAD_DIM), tv.reshape(n_seq, rows, HEAD_DIM),
                         cache_k, cache_v, page_table, sbg_rows, l, t_new)
        sb2 = sb2.reshape(n_seq, N_HEADS, t_new, HEAD_DIM).transpose(0, 2, 1, 3)
        sb2 = sb2.reshape(n_seq * t_new, GROUP_WIDTH)
        hgo2, s_s = _hgrn_decode(uq, uf, ui, ug, state_hgrn, hg_lb_logits, hgg, l)
        hmid2 = _outproj(sb2, hgo2.reshape(n_seq * t_new, GROUP_WIDTH), w_out_l, hs, tm_s, 1024)
        hs = _mlp(hmid2, g2, w_up_l, w_down_l, gf, tm_s, 1024, last)
        outs["ks"].append(tk.reshape(n_seq, t_new, N_HEADS, HEAD_DIM))
        outs["vs"].append(tv.reshape(n_seq, t_new, N_HEADS, HEAD_DIM))
        outs["ss"].append(s_s[0])

    y_prompt = hp.reshape(batch, seq, d)
    y_sample = hs.reshape(n_seq, t_new, d)
    return (y_prompt, y_sample, jnp.stack(outs["kp"]), jnp.stack(outs["vp"]), jnp.stack(outs["sp"]),
            jnp.stack(outs["ks"]), jnp.stack(outs["vs"]), jnp.stack(outs["ss"]))
```

```python
import functools

import jax
import jax.numpy as jnp
from jax import lax
from jax.experimental import pallas as pl
from jax.experimental.pallas import tpu as pltpu

F32 = jnp.float32
BF16 = jnp.bfloat16

RMS_EPS = 1e-6
HEAD_DIM = 128
N_HEADS = 8
GROUP_WIDTH = N_HEADS * HEAD_DIM
HG_CHUNK = 32
HG_SUB = 8
VMEM_LIMIT = 56 * 1024 * 1024
VMEM_LIMIT_FUSED = 60 * 1024 * 1024


def _params(*sem):
    return pltpu.CompilerParams(dimension_semantics=sem, vmem_limit_bytes=VMEM_LIMIT)


def _rms_normalize(x, g):
    return x * lax.rsqrt(jnp.mean(x * x, axis=-1, keepdims=True) + RMS_EPS) * g


def _split_hi_lo(x):
    hi = x.astype(BF16)
    lo = (x - hi.astype(F32)).astype(BF16)
    return hi, lo


def _dot_nt(a, b):
    return lax.dot_general(a, b, (((1,), (1,)), ((), ())), preferred_element_type=F32)


def _dot_tn(a, b):
    return lax.dot_general(a, b, (((0,), (0,)), ((), ())), preferred_element_type=F32)


def _sigmoid(x):
    return 1.0 / (1.0 + jnp.exp(-x))


def _silu(x):
    return x * _sigmoid(x)


INPROJ_F32_GROUPS = (1, 2, 4)
INPROJ_BF16_GROUPS = (0, 3, 5, 6)


def _inproj_kernel(order_ref, x_ref, g_ref, w_ref, *refs, parts):
    del order_ref
    f32_refs, narrow_ref, xn_ref = refs[:-2], refs[-2], refs[-1]
    j = pl.program_id(1)

    @pl.when(j == 0)
    def _():
        xn_ref[...] = _rms_normalize(x_ref[...], g_ref[...]).astype(BF16)

    for n, o_ref in enumerate(f32_refs):
        @pl.when(j // parts == n)
        def _(o_ref=o_ref):
            o_ref[...] = jnp.dot(xn_ref[...], w_ref[...], preferred_element_type=F32)

    @pl.when(j >= len(f32_refs) * parts)
    def _():
        narrow_ref[...] = jnp.dot(xn_ref[...], w_ref[...],
                                  preferred_element_type=F32).astype(BF16)


def _inproj(x, g, w_bf16, tm, tn):
    m, d = x.shape
    n_wide, n_narrow = len(INPROJ_F32_GROUPS), len(INPROJ_BF16_GROUPS)
    assert w_bf16.shape[1] == (n_wide + n_narrow) * GROUP_WIDTH and m % tm == 0
    parts = GROUP_WIDTH // tn
    order = jnp.array(INPROJ_F32_GROUPS + INPROJ_BF16_GROUPS, jnp.int32)

    def f32_spec(n):
        return pl.BlockSpec((tm, tn), lambda i, j, order: (i, jnp.clip(j - n * parts, 0, parts - 1)))

    def narrow_map(i, j, order):
        jn = jnp.maximum(j - n_wide * parts, 0)
        return (jn // parts, i, jn % parts)

    return pl.pallas_call(
        functools.partial(_inproj_kernel, parts=parts),
        grid_spec=pltpu.PrefetchScalarGridSpec(
            num_scalar_prefetch=1,
            grid=(m // tm, (n_wide + n_narrow) * parts),
            in_specs=[
                pl.BlockSpec((tm, d), lambda i, j, order: (i, 0)),
                pl.BlockSpec((1, d), lambda i, j, order: (0, 0)),
                pl.BlockSpec((d, tn), lambda i, j, order: (0, order[j // parts] * parts + j % parts)),
            ],
            out_specs=[f32_spec(n) for n in range(n_wide)]
            + [pl.BlockSpec((None, tm, tn), narrow_map)],
            scratch_shapes=[pltpu.VMEM((tm, d), BF16)],
        ),
        out_shape=[jax.ShapeDtypeStruct((m, GROUP_WIDTH), F32)] * n_wide
        + [jax.ShapeDtypeStruct((n_narrow, m, GROUP_WIDTH), BF16)],
        compiler_params=_params("parallel", "arbitrary"),
        name="inproj",
    )(order, x, g, w_bf16)


SB_SUM_PARTS = 1


def _suffix_matrix():
    shape = (SB_SUM_PARTS * HEAD_DIM, 2 * HEAD_DIM)
    r = lax.broadcasted_iota(jnp.int32, shape, 0) % HEAD_DIM
    c = lax.broadcasted_iota(jnp.int32, shape, 1)
    return jnp.where((c >= HEAD_DIM) | (r > c), 1.0, 0.0).astype(BF16)


LOG2_E = 1.4426950408889634


def _log_break(z2):
    l = jnp.log2(1.0 + jnp.exp2(-jnp.abs(z2)))
    log_beta = jnp.minimum(z2, 0.0) - l
    return log_beta, log_beta - z2


def _half_sums(log_keep, suffix_mat):
    parts = _split_hi_lo(log_keep) if suffix_mat.shape[0] == 2 * HEAD_DIM else (log_keep.astype(BF16),)
    rows, n = log_keep.shape[0], log_keep.shape[1] // HEAD_DIM
    stacked = []
    for s in range(n):
        sl = slice(s * HEAD_DIM, (s + 1) * HEAD_DIM)
        stacked.append(jnp.concatenate([p[:, sl] for p in parts], axis=-1))
    c = jnp.dot(jnp.concatenate(stacked, axis=0) if n > 1 else stacked[0], suffix_mat,
                preferred_element_type=F32)
    return [(c[s * rows:(s + 1) * rows, :HEAD_DIM], c[s * rows:(s + 1) * rows, HEAD_DIM:])
            for s in range(n)]


def _sb_sums(z2, suffix_mat, visible=None, keep=None):
    log_beta, log_keep = _log_break(z2)
    if keep is not None:
        log_keep = log_keep * keep
    if visible is not None:
        log_keep = jnp.where(visible, log_keep, 0.0)
    return log_beta, _half_sums(log_keep, suffix_mat)


def _sb_weights(log_beta, halves, carry, visible=None, keep=None):
    after = []
    for suffix, total in reversed(halves):
        after.append(suffix + carry)
        carry = carry + total
    after = jnp.concatenate(after[::-1], axis=-1) if len(after) > 1 else after[0]
    w = jnp.exp2(log_beta + after)
    if keep is not None:
        w = w * keep
    if visible is not None:
        w = jnp.where(visible, w, 0.0)
    return w, carry


def _sb_block(z2, carry, suffix_mat, **mask):
    log_beta, halves = _sb_sums(z2, suffix_mat, **mask)
    return _sb_weights(log_beta, halves, carry, **mask)


SB_HEADS_PER_STEP = 4


def _sb_prompt_kernel(bias_ref, q_ref, k_ref, v_ref, g_ref, o_ref, *, tq, tk):
    hb = pl.program_id(1)
    i = pl.program_id(2)
    scale2 = HEAD_DIM ** -0.5 * LOG2_E
    suffix_mat = _suffix_matrix()
    n_diag = tq // tk
    heads = [slice(n * HEAD_DIM, (n + 1) * HEAD_DIM) for n in range(SB_HEADS_PER_STEP)]
    qs = [q_ref[:, sl] for sl in heads]
    bias2 = [bias_ref[hb * SB_HEADS_PER_STEP + n] * LOG2_E for n in range(SB_HEADS_PER_STEP)]

    def block(j, state, visible):
        start = pl.multiple_of(j * tk, tk)
        n_heads = len(heads)
        z, sums, new = [None] * n_heads, [None] * n_heads, [None] * n_heads
        for step in range(n_heads + 2):
            n = step
            if n < n_heads:
                kb = (k_ref[pl.ds(start, tk), heads[n]] * scale2).astype(BF16)
                z[n] = _dot_nt(qs[n], kb) + bias2[n]
            n = step - 1
            if 0 <= n < n_heads:
                sums[n] = _sb_sums(z[n], suffix_mat, visible)
            n = step - 2
            if 0 <= n < n_heads:
                acc, carry = state[n]
                w, carry = _sb_weights(*sums[n], carry, visible)
                vb = v_ref[pl.ds(start, tk), heads[n]].astype(BF16)
                new[n] = (acc + jnp.dot(w.astype(BF16), vb, preferred_element_type=F32), carry)
        return tuple(new)

    row = lax.broadcasted_iota(jnp.int32, (tq, tk), 0)
    col = lax.broadcasted_iota(jnp.int32, (tq, tk), 1)
    zeros = jnp.zeros((tq, HEAD_DIM), F32)
    state = tuple((zeros, zeros) for _ in heads)
    for d in reversed(range(n_diag)):
        state = block(i * n_diag + d, state, col + d * tk < row)

    n_before = i * n_diag
    state = lax.fori_loop(0, n_before, lambda jj, st: block(n_before - 1 - jj, st, None), state)
    for n, sl in enumerate(heads):
        o_ref[:, sl] = _rms_normalize(state[n][0], g_ref[:, sl]).astype(o_ref.dtype)


def _narrow_spec(rows, width, group, index_map):
    slot = INPROJ_BF16_GROUPS.index(group)
    return pl.BlockSpec((None, rows, width), lambda *idx: (slot,) + tuple(index_map(*idx)))


def _sb_prompt(narrow, k, v, bias, g, batch, seq, tq, tk):
    nq = seq // tq
    width = SB_HEADS_PER_STEP * HEAD_DIM
    assert tq % tk == 0 and tk % HEAD_DIM == 0 and N_HEADS % SB_HEADS_PER_STEP == 0
    return pl.pallas_call(
        functools.partial(_sb_prompt_kernel, tq=tq, tk=tk),
        grid=(batch, N_HEADS // SB_HEADS_PER_STEP, nq),
        in_specs=[
            pl.BlockSpec(memory_space=pltpu.SMEM),
            _narrow_spec(tq, width, 0, lambda b, h, i: (b * nq + i, h)),
            pl.BlockSpec((seq, width), lambda b, h, i: (b, h)),
            pl.BlockSpec((seq, width), lambda b, h, i: (b, h)),
            pl.BlockSpec((1, width), lambda b, h, i: (0, h)),
        ],
        out_specs=pl.BlockSpec((tq, width), lambda b, h, i: (b * nq + i, h)),
        out_shape=jax.ShapeDtypeStruct((batch * seq, GROUP_WIDTH), BF16),
        compiler_params=_params("parallel", "parallel", "arbitrary"),
        name="sb_prompt",
    )(bias, narrow, k, v, g)


def _sb_decode_kernel(pt_ref, q_ref, bias_ref, kn_ref, vn_ref, *refs, t_new, pages_per_step,
                      seqs_per_step):
    del pt_ref
    stages = _sb_decode_stages(pl.program_id(1), pl.num_programs(1), q_ref, bias_ref, kn_ref, vn_ref,
                               *refs, t_new=t_new, pages_per_step=pages_per_step,
                               seqs_per_step=seqs_per_step)
    for stage in stages:
        stage()


def _sb_decode_stages(p, n_steps, q_ref, bias_ref, kn_ref, vn_ref, *refs, t_new, pages_per_step,
                      seqs_per_step):
    n_page_refs = seqs_per_step * pages_per_step
    k_refs, v_refs = refs[:n_page_refs], refs[n_page_refs:2 * n_page_refs]
    g_ref, o_ref, acc_ref, carry_ref = refs[2 * n_page_refs:]
    rows = N_HEADS * t_new
    page_cols = k_refs[0].shape[2] * N_HEADS
    scale2 = HEAD_DIM ** -0.5 * LOG2_E
    suffix_mat = _suffix_matrix()
    seqs = range(seqs_per_step)

    def tiled(x, n):
        return jnp.concatenate([x] * n, axis=-1) if n > 1 else x

    def logits(keys):
        bias2 = tiled(bias_ref[...] * LOG2_E, keys[0].shape[0] // HEAD_DIM)
        return [_dot_nt(q_ref[s], keys[s].astype(BF16)) * scale2 + bias2 for s in seqs]

    def weigh(sums, values, acc, carry, **mask):
        out = []
        for s in seqs:
            w, new_carry = _sb_weights(*sums[s], carry[s], **mask)
            out.append((acc[s] + jnp.dot(w.astype(BF16), values[s].astype(BF16),
                                         preferred_element_type=F32), new_carry))
        return [o[0] for o in out], [o[1] for o in out]

    r = lax.broadcasted_iota(jnp.int32, (rows, HEAD_DIM), 0)
    c = lax.broadcasted_iota(jnp.int32, (rows, HEAD_DIM), 1)
    heads_agree = (c % N_HEADS) == (r // t_new)
    keep = tiled(jnp.where(heads_agree, 1.0, 0.0), page_cols * pages_per_step // HEAD_DIM)

    def pages(page_refs, s):
        mine = page_refs[s * pages_per_step:(s + 1) * pages_per_step]
        return jnp.concatenate([ref[0, 0].reshape(page_cols, HEAD_DIM) for ref in mine[::-1]], axis=0)

    live = {}

    def stage_logits():
        @pl.when(p == 0)
        def _():
            visible = heads_agree & (c // N_HEADS < r % t_new)
            pad = jnp.zeros((HEAD_DIM - rows, HEAD_DIM), F32)
            kn = [jnp.concatenate([kn_ref[s], pad], axis=0) for s in seqs]
            vn = [jnp.concatenate([vn_ref[s], pad], axis=0) for s in seqs]
            zeros = [jnp.zeros((rows, HEAD_DIM), F32)] * seqs_per_step
            sums = [_sb_sums(z, suffix_mat, visible=visible) for z in logits(kn)]
            acc, carry = weigh(sums, vn, zeros, zeros, visible=visible)
            for s in seqs:
                acc_ref[s] = acc[s]
                carry_ref[s] = carry[s]

        live["z2"] = logits([pages(k_refs, s) for s in seqs])

    def stage_sums():
        live["sums"] = [_sb_sums(z, suffix_mat, keep=keep) for z in live["z2"]]

    def stage_weights():
        acc, carry = weigh(live["sums"], [pages(v_refs, s) for s in seqs],
                           [acc_ref[s] for s in seqs], [carry_ref[s] for s in seqs], keep=keep)
        for s in seqs:
            acc_ref[s] = acc[s]
            carry_ref[s] = carry[s]

        @pl.when(p == n_steps - 1)
        def _():
            for s in seqs:
                o_ref[s] = _rms_normalize(acc[s], g_ref[...])

    return stage_logits, stage_sums, stage_weights


def _sb_decode(q_rows, bias_rows, k_new, v_new, cache_k, cache_v, page_table, g_rows, layer, t_new):
    plan = _sb_decode_plan(q_rows, bias_rows, k_new, v_new, cache_k, cache_v, page_table, g_rows,
                           layer, t_new, locate=lambda sb, p: (sb, p))
    return pl.pallas_call(
        functools.partial(_sb_decode_kernel, **plan["static"]),
        grid_spec=pltpu.PrefetchScalarGridSpec(
            num_scalar_prefetch=1,
            grid=plan["grid"],
            in_specs=plan["in_specs"],
            out_specs=plan["out_spec"],
            scratch_shapes=plan["scratch"],
        ),
        out_shape=plan["out_shape"],
        compiler_params=_params("parallel", "arbitrary"),
        name="sb_decode",
    )(page_table.reshape(-1), *plan["operands"])


def _sb_decode_plan(q_rows, bias_rows, k_new, v_new, cache_k, cache_v, page_table, g_rows, layer,
                    t_new, locate):
    n_seq, rows, _ = q_rows.shape
    n_pages = page_table.shape[1]
    page = cache_k.shape[2]
    assert cache_k.shape[3:] == (N_HEADS, HEAD_DIM) and rows <= HEAD_DIM
    pages_per_step = next(n for n in (4, 2, 1) if n_pages % n == 0)
    seqs_per_step = next(n for n in (2, 1) if n_seq % n == 0)
    row_spec = pl.BlockSpec((seqs_per_step, rows, HEAD_DIM),
                            lambda *idx: (locate(*idx[:-1])[0], 0, 0))
    const_spec = pl.BlockSpec((rows, HEAD_DIM), lambda *idx: (0, 0))

    def page_spec(s, i):
        def index_map(*idx):
            sb, p = locate(*idx[:-1])
            seq = sb * seqs_per_step + s
            return (layer, idx[-1][seq * n_pages + n_pages - 1 - (p * pages_per_step + i)], 0, 0, 0)
        return pl.BlockSpec((1, 1, page, N_HEADS, HEAD_DIM), index_map)

    page_specs = [page_spec(s, i) for s in range(seqs_per_step) for i in range(pages_per_step)]
    scratch = pltpu.VMEM((seqs_per_step, rows, HEAD_DIM), F32)
    return dict(
        grid=(n_seq // seqs_per_step, n_pages // pages_per_step),
        in_specs=[row_spec, const_spec, row_spec, row_spec] + page_specs + page_specs + [const_spec],
        operands=[q_rows, bias_rows, k_new, v_new] + [cache_k] * len(page_specs)
        + [cache_v] * len(page_specs) + [g_rows],
        out_spec=row_spec,
        out_shape=jax.ShapeDtypeStruct((n_seq, rows, HEAD_DIM), F32),
        scratch=[scratch, scratch],
        static=dict(t_new=t_new, pages_per_step=pages_per_step, seqs_per_step=seqs_per_step),
    )


def _lower_bound(logits, layer):
    e = jnp.exp(logits - jnp.max(logits, axis=0, keepdims=True))
    return jnp.sum(e[:layer + 1], axis=0, keepdims=True) / jnp.sum(e, axis=0, keepdims=True)


def _hgrn_gates(hq, hf, lb, log=jnp.log):
    f = lb + (1.0 - lb) * _sigmoid(hf)
    return _silu(hq), 1.0 - f, log(f)


def _block_tril(n, blk):
    r = lax.broadcasted_iota(jnp.int32, (n, n), 0)
    c = lax.broadcasted_iota(jnp.int32, (n, n), 1)
    return jnp.where((c <= r) & (r // blk == c // blk), 1.0, 0.0).astype(BF16)


HG_HEADS_PER_STEP = 4


def _hgrn_prompt_kernel(hq_ref, hf_ref, hi_ref, hg_ref, lbl_ref, g_ref, o_ref, s_ref, st_ref,
                        *, tc, layer):
    t = pl.program_id(2)
    n_chunks = tc // HG_CHUNK
    n_sub = HG_CHUNK // HG_SUB
    heads = [slice(n * HEAD_DIM, (n + 1) * HEAD_DIM) for n in range(HG_HEADS_PER_STEP)]
    every = range(len(heads))

    @pl.when(t == 0)
    def _():
        st_ref[...] = jnp.zeros_like(st_ref)

    tril_chunk = _block_tril(tc, HG_CHUNK)
    tril_sub = _block_tril(tc, HG_SUB)
    rr =lax.broadcasted_iota(jnp.int32, (2 * HEAD_DIM, 2 * HEAD_DIM), 0) // HEAD_DIM
    cc = lax.broadcasted_iota(jnp.int32, (2 * HEAD_DIM, 2 * HEAD_DIM), 1) // HEAD_DIM
    pair_ones = jnp.where(rr == cc, 1.0, 0.0).astype(BF16)
    sub_shape = (tc // HG_SUB, HG_SUB, HEAD_DIM)
    chunk_shape = (n_chunks, HG_CHUNK, HEAD_DIM)
    r_sub = lax.broadcasted_iota(jnp.int32, sub_shape, 1)
    r_chunk = lax.broadcasted_iota(jnp.int32, chunk_shape, 1)

    q, kk, g, v = [], [], [], []
    for sl in heads:
        lb = _lower_bound(lbl_ref[:, sl], layer)
        qh, kh, gh = _hgrn_gates(hq_ref[:, sl].astype(F32), hf_ref[:, sl], lb, jnp.log2)
        q.append(qh)
        kk.append(kh)
        g.append(gh)
        v.append(hi_ref[:, sl].astype(F32))
    v_bf = [x.astype(BF16) for x in v]

    b, c = [], []
    for n in every:
        g_hi, g_lo = _split_hi_lo(g[n])
        b.append(jnp.dot(tril_chunk, g_hi, preferred_element_type=F32)
                 + jnp.dot(tril_chunk, g_lo, preferred_element_type=F32))
        c.append(jnp.dot(tril_sub, g_hi, preferred_element_type=F32)
                 + jnp.dot(tril_sub, g_lo, preferred_element_type=F32))

    scores, shifted_v = [], []
    for n in every:
        q3, k3, c3, v3 = (a.reshape(sub_shape) for a in (q[n], kk[n], c[n], v[n]))
        prods, rolled = [q3 * k3], [v3]
        for delta in range(1, HG_SUB):
            ks = pltpu.roll(k3, delta, axis=1)
            cs = pltpu.roll(c3, delta, axis=1)
            prods.append(jnp.where(r_sub >= delta, q3 * ks * jnp.exp2(c3 - cs), 0.0))
            rolled.append(pltpu.roll(v3, delta, axis=1))
        prods = [x.reshape(tc, HEAD_DIM).astype(BF16) for x in prods]
        paired = [jnp.concatenate(prods[d:d + 2], axis=-1) for d in range(0, HG_SUB, 2)]
        scores.append(jnp.dot(jnp.concatenate(paired, axis=0), pair_ones,
                              preferred_element_type=F32))
        shifted_v.append(rolled)
    o = []
    for n in every:
        acc = jnp.zeros((tc, HEAD_DIM), F32)
        for delta in range(HG_SUB):
            rows = slice((delta // 2) * tc, (delta // 2 + 1) * tc)
            cols = slice((delta % 2) * HEAD_DIM, (delta % 2 + 1) * HEAD_DIM)
            acc = acc + scores[n][rows, cols] * shifted_v[n][delta].reshape(tc, HEAD_DIM)
        o.append(acc)

    for n in every:
        b3 = b[n].reshape(chunk_shape)
        kk3 = kk[n].reshape(chunk_shape)
        q_sub = (q[n] * jnp.exp2(c[n])).reshape(chunk_shape)
        q_cat, k_cat = [], []
        for i in range(1, n_sub):
            lo_row = i * HG_SUB
            b_start = b3[:, lo_row - 1:lo_row, :]
            k_cat.append(jnp.where(r_chunk < lo_row,
                                   kk3 * jnp.exp2(jnp.minimum(b_start - b3, 0.0)), 0.0).astype(BF16))
            q_cat.append(jnp.where((r_chunk >= lo_row) & (r_chunk < lo_row + HG_SUB),
                                   q_sub, 0.0).astype(BF16))
        a_off = jnp.einsum('ctk,csk->cts', jnp.concatenate(q_cat, axis=-1),
                           jnp.concatenate(k_cat, axis=-1), preferred_element_type=F32)
        o_off = jnp.einsum('cts,csv->ctv', a_off.astype(BF16), v_bf[n].reshape(chunk_shape),
                           preferred_element_type=F32)
        o[n] = o[n] + o_off.reshape(tc, HEAD_DIM)

    q_in = [(q[n] * jnp.exp2(b[n])).astype(BF16) for n in every]
    st = [st_ref[n] for n in every]
    o_inter = [[] for _ in every]
    for ci in range(n_chunks):
        rows = slice(ci * HG_CHUNK, (ci + 1) * HG_CHUNK)
        for n in every:
            b_c = b[n][rows]
            b_last = b_c[HG_CHUNK - 1:HG_CHUNK]
            o_inter[n].append(_dot_nt(q_in[n][rows], st[n].astype(BF16)))
            k_out = (kk[n][rows] * jnp.exp2(b_last - b_c)).astype(BF16)
            st[n] = st[n] * jnp.exp2(b_last) + _dot_tn(v_bf[n][rows], k_out)

    for n, sl in enumerate(heads):
        st_ref[n] = st[n]
        total = o[n] + jnp.concatenate(o_inter[n], axis=0)
        gate = _silu(hg_ref[:, sl].astype(F32))
        o_ref[:, sl] = (_rms_normalize(total, g_ref[:, sl]) * gate).astype(o_ref.dtype)

    @pl.when(t == pl.num_programs(2) - 1)
    def _():
        for n in every:
            s_ref[0, n] = st[n].T


def _hgrn_prompt(narrow, hf, lb_logits, g, batch, seq, tc, layer):
    nt = seq // tc
    n_layers = lb_logits.shape[0]
    width = HG_HEADS_PER_STEP * HEAD_DIM
    assert N_HEADS % HG_HEADS_PER_STEP == 0
    tok = lambda b, h, t: (b * nt + t, h)
    return pl.pallas_call(
        functools.partial(_hgrn_prompt_kernel, tc=tc, layer=layer),
        grid=(batch, N_HEADS // HG_HEADS_PER_STEP, nt),
        in_specs=[
            _narrow_spec(tc, width, 3, tok),
            pl.BlockSpec((tc, width), tok),
            _narrow_spec(tc, width, 5, tok),
            _narrow_spec(tc, width, 6, tok),
            pl.BlockSpec((n_layers, width), lambda b, h, t: (0, h)),
            pl.BlockSpec((1, width), lambda b, h, t: (0, h)),
        ],
        out_specs=[
            pl.BlockSpec((tc, width), tok),
            pl.BlockSpec((1, HG_HEADS_PER_STEP, HEAD_DIM, HEAD_DIM), lambda b, h, t: (b, h, 0, 0)),
        ],
        out_shape=[
            jax.ShapeDtypeStruct((batch * seq, GROUP_WIDTH), BF16),
            jax.ShapeDtypeStruct((batch, N_HEADS, HEAD_DIM, HEAD_DIM), F32),
        ],
        scratch_shapes=[pltpu.VMEM((HG_HEADS_PER_STEP, HEAD_DIM, HEAD_DIM), F32)],
        compiler_params=_params("parallel", "parallel", "arbitrary"),
        name="hgrn_prompt",
    )(narrow, hf, narrow, narrow, lb_logits, g)


def _hgrn_decode_kernel(hq_ref, hf_ref, hi_ref, hg_ref, s_ref, lbl_ref, g_ref, o_ref, so_ref,
                        *, t_new, layer, seqs_per_step):
    heads = [slice(h * HEAD_DIM, (h + 1) * HEAD_DIM) for h in range(N_HEADS)]
    pairs = [(t, u) for t in range(t_new) for u in range(t + 1)]
    pair_rows = -(-len(pairs) // 16) * 16
    ones = jnp.ones((HEAD_DIM, HEAD_DIM), BF16)
    lb = _lower_bound(lbl_ref[...], layer)
    for s in range(seqs_per_step):
        q, kk, g = _hgrn_gates(hq_ref[s].astype(F32), hf_ref[s], lb, jnp.log2)
        v = hi_ref[s].astype(F32)
        b_rows = [g[0:1]]
        for t in range(1, t_new):
            b_rows.append(b_rows[-1] + g[t:t + 1])
        b_last = b_rows[-1]
        b = jnp.concatenate(b_rows, axis=0)

        prods = [q[t:t + 1] * kk[u:u + 1] * jnp.exp2(b_rows[t] - b_rows[u]) for t, u in pairs]
        prods.append(jnp.zeros((pair_rows - len(pairs), GROUP_WIDTH), F32))
        prods = jnp.concatenate(prods, axis=0)
        a = jnp.dot(jnp.concatenate([prods[:, sl] for sl in heads], axis=0).astype(BF16), ones,
                    preferred_element_type=F32)

        q_in = (q * jnp.exp2(b)).astype(BF16)
        pad = jnp.zeros((HG_SUB - t_new, GROUP_WIDTH), F32)
        k_out = jnp.concatenate([kk * jnp.exp2(b_last - b), pad], axis=0).astype(BF16)
        v_pad = jnp.concatenate([v, pad], axis=0).astype(BF16)
        decay_t = jnp.concatenate([jnp.exp2(b_last[:, sl]) for sl in heads], axis=0).T
        gate = _silu(hg_ref[s].astype(F32))
        for h, sl in enumerate(heads):
            state = s_ref[0, s, h]
            o = jnp.dot(q_in[:, sl], state.astype(BF16), preferred_element_type=F32)
            rows = []
            for t in range(t_new):
                row = o[t:t + 1]
                for u in range(t + 1):
                    i = h * pair_rows + pairs.index((t, u))
                    row = row + a[i:i + 1] * v[u:u + 1, sl]
                rows.append(row)
            o = jnp.concatenate(rows, axis=0)
            o_ref[s, :, sl] = _rms_normalize(o, g_ref[:, sl]) * gate[:, sl]
            so_ref[0, s, h] = state * decay_t[:, h:h + 1] + _dot_tn(k_out[:, sl], v_pad[:, sl])


def _hgrn_decode(hq, hf, hi, hg, state, lb_logits, g, layer):
    n_seq, t_new, _ = hq.shape
    assert t_new <= HG_SUB
    n_layers = lb_logits.shape[0]
    seqs_per_step = next(n for n in (2, 1) if n_seq % n == 0)
    row_spec = pl.BlockSpec((seqs_per_step, t_new, GROUP_WIDTH), lambda s: (s, 0, 0))
    state_block = (1, seqs_per_step, N_HEADS, HEAD_DIM, HEAD_DIM)
    return pl.pallas_call(
        functools.partial(_hgrn_decode_kernel, t_new=t_new, layer=layer,
                          seqs_per_step=seqs_per_step),
        grid=(n_seq // seqs_per_step,),
        in_specs=[
            row_spec, row_spec, row_spec, row_spec,
            pl.BlockSpec(state_block, lambda s: (layer, s, 0, 0, 0)),
            pl.BlockSpec((n_layers, GROUP_WIDTH), lambda s: (0, 0)),
            pl.BlockSpec((1, GROUP_WIDTH), lambda s: (0, 0)),
        ],
        out_specs=[
            row_spec,
            pl.BlockSpec(state_block, lambda s: (0, s, 0, 0, 0)),
        ],
        out_shape=[
            jax.ShapeDtypeStruct((n_seq, t_new, GROUP_WIDTH), F32),
            jax.ShapeDtypeStruct((1, n_seq, N_HEADS, HEAD_DIM, HEAD_DIM), F32),
        ],
        compiler_params=_params("parallel"),
        name="hgrn_decode",
    )(hq, hf, hi, hg, state, lb_logits, g)


def _outproj_kernel(sb_ref, hg_ref, w_ref, x_ref, o_ref):
    half = sb_ref.shape[1]
    acc = jnp.dot(sb_ref[...].astype(BF16), w_ref[:half, :], preferred_element_type=F32)
    acc = acc + jnp.dot(hg_ref[...].astype(BF16), w_ref[half:, :], preferred_element_type=F32)
    o_ref[...] = x_ref[...] + acc


def _outproj(sb, hg, w_bf16, x, tm, tn):
    m, d = x.shape
    width = sb.shape[1]
    return pl.pallas_call(
        _outproj_kernel,
        grid=(m // tm, d // tn),
        in_specs=[
            pl.BlockSpec((tm, width), lambda i, j: (i, 0)),
            pl.BlockSpec((tm, width), lambda i, j: (i, 0)),
            pl.BlockSpec((2 * width, tn), lambda i, j: (0, j)),
            pl.BlockSpec((tm, tn), lambda i, j: (i, j)),
        ],
        out_specs=pl.BlockSpec((tm, tn), lambda i, j: (i, j)),
        out_shape=jax.ShapeDtypeStruct((m, d), F32),
        compiler_params=_params("parallel", "arbitrary"),
        name="outproj",
    )(sb, hg, w_bf16, x)


def _mlp_kernel(h_ref, g2_ref, wu_ref, wd_ref, gf_ref, y_ref, hn_ref, *, final_norm):
    j = pl.program_id(1)

    @pl.when(j == 0)
    def _():
        hn_ref[...] = _rms_normalize(h_ref[...], g2_ref[...]).astype(BF16)
        y_ref[...] = jnp.zeros_like(y_ref)

    u = jnp.dot(hn_ref[...], wu_ref[...], preferred_element_type=F32)
    a = jnp.square(jnp.maximum(u, 0.0)).astype(BF16)
    y_ref[...] += jnp.dot(a, wd_ref[...], preferred_element_type=F32)

    @pl.when(j == pl.num_programs(1) - 1)
    def _():
        y = h_ref[...] + y_ref[...]
        y_ref[...] = _rms_normalize(y, gf_ref[...]) if final_norm else y


def _mlp(h, g2, wu_bf16, wd_bf16, gf, tm, tf, final_norm):
    m, d = h.shape
    f = wu_bf16.shape[1]
    return pl.pallas_call(
        functools.partial(_mlp_kernel, final_norm=final_norm),
        grid=(m // tm, f // tf),
        in_specs=[
            pl.BlockSpec((tm, d), lambda i, j: (i, 0)),
            pl.BlockSpec((1, d), lambda i, j: (0, 0)),
            pl.BlockSpec((d, tf), lambda i, j: (0, j)),
            pl.BlockSpec((tf, d), lambda i, j: (j, 0)),
            pl.BlockSpec((1, d), lambda i, j: (0, 0)),
        ],
        out_specs=pl.BlockSpec((tm, d), lambda i, j: (i, 0)),
        out_shape=jax.ShapeDtypeStruct((m, d), F32),
        scratch_shapes=[pltpu.VMEM((tm, d), BF16)],
        compiler_params=_params("parallel", "arbitrary"),
        name="mlp",
    )(h, g2, wu_bf16, wd_bf16, gf)


def _mlp_decode_kernel(pt_ref, h_ref, g2_ref, wu_ref, wd_ref, gf_ref, *refs, final_norm,
                       page_steps, **decode_static):
    del pt_ref
    decode_refs, y_ref, sb_ref = refs[:-5], refs[-5], refs[-4]
    hn_ref, dec_acc_ref, dec_carry_ref = refs[-3:]
    j = pl.program_id(1)
    step = pl.program_id(0) * pl.num_programs(1) + j
    logits, sums, weights = _sb_decode_stages(step % page_steps, page_steps, *decode_refs, sb_ref,
                                              dec_acc_ref, dec_carry_ref, **decode_static)

    @pl.when(j == 0)
    def _():
        hn_ref[...] = _rms_normalize(h_ref[...], g2_ref[...]).astype(BF16)
        y_ref[...] = jnp.zeros_like(y_ref)

    logits()
    u = jnp.dot(hn_ref[...], wu_ref[...], preferred_element_type=F32)
    sums()
    a = jnp.square(jnp.maximum(u, 0.0)).astype(BF16)
    y_ref[...] += jnp.dot(a, wd_ref[...], preferred_element_type=F32)
    weights()

    @pl.when(j == pl.num_programs(1) - 1)
    def _():
        y = h_ref[...] + y_ref[...]
        y_ref[...] = _rms_normalize(y, gf_ref[...]) if final_norm else y


def _mlp_decode_fits(m, f, tm, tf, n_seq, n_pages):
    pages_per_step = next(n for n in (4, 2, 1) if n_pages % n == 0)
    seqs_per_step = next(n for n in (2, 1) if n_seq % n == 0)
    return (m // tm) * (f // tf) == (n_seq // seqs_per_step) * (n_pages // pages_per_step)


def _mlp_decode(h, g2, wu_bf16, wd_bf16, gf, tm, tf, final_norm, *decode_args):
    m, d = h.shape
    f = wu_bf16.shape[1]
    nj = f // tf
    page_table = decode_args[6]
    n_pages = page_table.shape[1]

    def locate(i, j):
        step = i * nj + j
        return step // page_steps, step % page_steps

    plan = _sb_decode_plan(*decode_args, locate=locate)
    page_steps = plan["grid"][1]
    assert (m // tm) * nj == plan["grid"][0] * page_steps
    return pl.pallas_call(
        functools.partial(_mlp_decode_kernel, final_norm=final_norm, page_steps=page_steps,
                          **plan["static"]),
        grid_spec=pltpu.PrefetchScalarGridSpec(
            num_scalar_prefetch=1,
            grid=(m // tm, nj),
            in_specs=[
                pl.BlockSpec((tm, d), lambda i, j, pt: (i, 0)),
                pl.BlockSpec((1, d), lambda i, j, pt: (0, 0)),
                pl.BlockSpec((d, tf), lambda i, j, pt: (0, j)),
                pl.BlockSpec((tf, d), lambda i, j, pt: (j, 0)),
                pl.BlockSpec((1, d), lambda i, j, pt: (0, 0)),
            ] + plan["in_specs"],
            out_specs=[pl.BlockSpec((tm, d), lambda i, j, pt: (i, 0)), plan["out_spec"]],
            scratch_shapes=[pltpu.VMEM((tm, d), BF16)] + plan["scratch"],
        ),
        out_shape=[jax.ShapeDtypeStruct((m, d), F32), plan["out_shape"]],
        compiler_params=pltpu.CompilerParams(dimension_semantics=("arbitrary", "arbitrary"),
                                             vmem_limit_bytes=VMEM_LIMIT_FUSED),
        name="mlp_decode",
    )(page_table.reshape(-1), h, g2, wu_bf16, wd_bf16, gf, *plan["operands"])


def _row_tile(m, target):
    return target if m % target == 0 else m


def kernel(x_prompt, x_sample, cache_k, cache_v, state_hgrn, page_table, norm1_g, w_in, sb_bias,
           sb_norm_g, hg_norm_g, hg_lb_logits, w_out, norm2_g, w_up, w_down, final_norm_g):
    batch, seq, d = x_prompt.shape
    n_seq, t_new, _ = x_sample.shape
    depth = w_in.shape[0]
    hp = x_prompt.reshape(batch * seq, d)
    hs = x_sample.reshape(n_seq * t_new, d)
    tm_in_p = _row_tile(batch * seq, 1024)
    tm_out_p = _row_tile(batch * seq, 1024)
    tm_p = _row_tile(batch * seq, 512)
    tm_s = _row_tile(n_seq * t_new, 512)
    tn_in = 512
    tf = 1024
    tq =_row_tile(seq, 512)
    tk_sb = _row_tile(tq, 256)
    tc = _row_tile(seq, 256)
    gf = final_norm_g.reshape(1, d)

    outs = {k: [] for k in ("kp", "vp", "sp", "ks", "vs", "ss")}
    for l in range(depth):
        g1 = norm1_g[l].reshape(1, d)
        g2 = norm2_g[l].reshape(1, d)
        sbg = sb_norm_g[l].reshape(1, GROUP_WIDTH)
        hgg = hg_norm_g[l].reshape(1, GROUP_WIDTH)
        w_in_l = w_in[l].astype(BF16)
        w_out_l = w_out[l].astype(BF16)
        w_up_l = w_up[l].astype(BF16)
        w_down_l = w_down[l].astype(BF16)
        last = l == depth - 1

        sk, sv, hf, narrow = _inproj(hp, g1, w_in_l, tm_in_p, tn_in)
        sb = _sb_prompt(narrow, sk, sv, sb_bias[l], sbg, batch, seq, tq, tk_sb)
        hgo, s_p = _hgrn_prompt(narrow, hf, hg_lb_logits, hgg, batch, seq, tc, l)
        hmid = _outproj(sb, hgo, w_out_l, hp, tm_out_p, 1024)
        outs["kp"].append(sk.reshape(batch, seq, N_HEADS, HEAD_DIM))
        outs["vp"].append(sv.reshape(batch, seq, N_HEADS, HEAD_DIM))
        outs["sp"].append(s_p)

        *wide_s, narrow_s = _inproj(hs, g1, w_in_l, tm_s, tn_in)
        tk, tv, uf = (a.reshape(n_seq, t_new, GROUP_WIDTH) for a in wide_s)
        tq_, uq, ui, ug = (narrow_s[n].reshape(n_seq, t_new, GROUP_WIDTH) for n in range(4))
        rows = N_HEADS * t_new
        sbg_rows = jnp.repeat(sbg.reshape(N_HEADS, HEAD_DIM), t_new, axis=0)
        bias_rows = jnp.broadcast_to(jnp.repeat(sb_bias[l], t_new)[:, None], (rows, HEAD_DIM))
        q_rows = tq_.reshape(n_seq, t_new, N_HEADS, HEAD_DIM).transpose(0, 2, 1, 3)
        decode_args = (q_rows.reshape(n_seq, rows, HEAD_DIM), bias_rows,
                       tk.reshape(n_seq, rows, HEAD_DIM), tv.reshape(n_seq, rows, HEAD_DIM),
                       cache_k, cache_v, page_table, sbg_rows, l, t_new)
        if _mlp_decode_fits(batch * seq, w_up_l.shape[1], tm_p, tf, n_seq, page_table.shape[1]):
            hp, sb2 = _mlp_decode(hmid, g2, w_up_l, w_down_l, gf, tm_p, tf, last, *decode_args)
        else:
            hp = _mlp(hmid, g2, w_up_l, w_down_l, gf, tm_p, tf, last)
            sb2 = _sb_decode(*decode_args)
        sb2 = sb2.reshape(n_seq, N_HEADS, t_new, HEAD_DIM).transpose(0, 2, 1, 3)
        sb2 = sb2.reshape(n_seq * t_new, GROUP_WIDTH)
        hgo2, s_s = _hgrn_decode(uq, uf, ui, ug, state_hgrn, hg_lb_logits, hgg, l)
        hmid2 = _outproj(sb2, hgo2.reshape(n_seq * t_new, GROUP_WIDTH), w_out_l, hs, tm_s, 1024)
        hs = _mlp(hmid2, g2, w_up_l, w_down_l, gf, tm_s, tf, last)
        outs["ks"].append(tk.reshape(n_seq, t_new, N_HEADS, HEAD_DIM))
        outs["vs"].append(tv.reshape(n_seq, t_new, N_HEADS, HEAD_DIM))
        outs["ss"].append(s_s[0])

    y_prompt = hp.reshape(batch, seq, d)
    y_sample = hs.reshape(n_seq, t_new, d)
    return (y_prompt, y_sample, jnp.stack(outs["kp"]), jnp.stack(outs["vp"]), jnp.stack(outs["sp"]),
            jnp.stack(outs["ks"]), jnp.stack(outs["vs"]), jnp.stack(outs["ss"]))
```

```python
import functools

import jax
import jax.numpy as jnp
from jax import lax
from jax.experimental import pallas as pl
from jax.experimental.pallas import tpu as pltpu

F32 = jnp.float32
BF16 = jnp.bfloat16

RMS_EPS = 1e-6
HEAD_DIM = 128
N_HEADS = 8
GROUP_WIDTH = N_HEADS * HEAD_DIM
HG_CHUNK = 32
HG_SUB = 8
VMEM_LIMIT = 56 * 1024 * 1024
VMEM_LIMIT_FUSED = 60 * 1024 * 1024


def _params(*sem):
    return pltpu.CompilerParams(dimension_semantics=sem, vmem_limit_bytes=VMEM_LIMIT)


def _rms_normalize(x, g):
    return x * lax.rsqrt(jnp.mean(x * x, axis=-1, keepdims=True) + RMS_EPS) * g


def _split_hi_lo(x):
    hi = x.astype(BF16)
    lo = (x - hi.astype(F32)).astype(BF16)
    return hi, lo


def _dot_nt(a, b):
    return lax.dot_general(a, b, (((1,), (1,)), ((), ())), preferred_element_type=F32)


def _dot_tn(a, b):
    return lax.dot_general(a, b, (((0,), (0,)), ((), ())), preferred_element_type=F32)


def _sigmoid(x):
    return 1.0 / (1.0 + jnp.exp(-x))


def _silu(x):
    return x * _sigmoid(x)


INPROJ_F32_GROUPS = (1, 2, 4)
INPROJ_BF16_GROUPS = (0, 3, 5, 6)


def _inproj_kernel(order_ref, x_ref, g_ref, w_ref, *refs, parts):
    del order_ref
    f32_refs, narrow_ref, xn_ref = refs[:-2], refs[-2], refs[-1]
    j = pl.program_id(1)

    @pl.when(j == 0)
    def _():
        xn_ref[...] = _rms_normalize(x_ref[...], g_ref[...]).astype(BF16)

    for n, o_ref in enumerate(f32_refs):
        @pl.when(j // parts == n)
        def _(o_ref=o_ref):
            o_ref[...] = jnp.dot(xn_ref[...], w_ref[...].astype(BF16), preferred_element_type=F32)

    @pl.when(j >= len(f32_refs) * parts)
    def _():
        narrow_ref[...] = jnp.dot(xn_ref[...], w_ref[...].astype(BF16),
                                  preferred_element_type=F32).astype(BF16)


def _inproj(x, g, w, tm, tn):
    m, d = x.shape
    n_wide, n_narrow = len(INPROJ_F32_GROUPS), len(INPROJ_BF16_GROUPS)
    assert w.shape[1] == (n_wide + n_narrow) * GROUP_WIDTH and m % tm == 0
    parts = GROUP_WIDTH // tn
    order = jnp.array(INPROJ_F32_GROUPS + INPROJ_BF16_GROUPS, jnp.int32)

    def f32_spec(n):
        return pl.BlockSpec((tm, tn), lambda i, j, order: (i, jnp.clip(j - n * parts, 0, parts - 1)))

    def narrow_map(i, j, order):
        jn = jnp.maximum(j - n_wide * parts, 0)
        return (jn // parts, i, jn % parts)

    return pl.pallas_call(
        functools.partial(_inproj_kernel, parts=parts),
        grid_spec=pltpu.PrefetchScalarGridSpec(
            num_scalar_prefetch=1,
            grid=(m // tm, (n_wide + n_narrow) * parts),
            in_specs=[
                pl.BlockSpec((tm, d), lambda i, j, order: (i, 0)),
                pl.BlockSpec((1, d), lambda i, j, order: (0, 0)),
                pl.BlockSpec((d, tn), lambda i, j, order: (0, order[j // parts] * parts + j % parts)),
            ],
            out_specs=[f32_spec(n) for n in range(n_wide)]
            + [pl.BlockSpec((None, tm, tn), narrow_map)],
            scratch_shapes=[pltpu.VMEM((tm, d), BF16)],
        ),
        out_shape=[jax.ShapeDtypeStruct((m, GROUP_WIDTH), F32)] * n_wide
        + [jax.ShapeDtypeStruct((n_narrow, m, GROUP_WIDTH), BF16)],
        compiler_params=_params("parallel", "arbitrary"),
        name="inproj",
    )(order, x, g, w)


SB_SUM_PARTS = 1


def _suffix_matrix():
    shape = (SB_SUM_PARTS * HEAD_DIM, 2 * HEAD_DIM)
    r = lax.broadcasted_iota(jnp.int32, shape, 0) % HEAD_DIM
    c = lax.broadcasted_iota(jnp.int32, shape, 1)
    return jnp.where((c >= HEAD_DIM) | (r > c), 1.0, 0.0).astype(BF16)


LOG2_E = 1.4426950408889634


def _log_break(z2):
    l = jnp.log2(1.0 + jnp.exp2(-jnp.abs(z2)))
    log_beta = jnp.minimum(z2, 0.0) - l
    return log_beta, log_beta - z2


def _half_sums(log_keep, suffix_mat):
    parts = _split_hi_lo(log_keep) if suffix_mat.shape[0] == 2 * HEAD_DIM else (log_keep.astype(BF16),)
    rows, n = log_keep.shape[0], log_keep.shape[1] // HEAD_DIM
    stacked = []
    for s in range(n):
        sl = slice(s * HEAD_DIM, (s + 1) * HEAD_DIM)
        stacked.append(jnp.concatenate([p[:, sl] for p in parts], axis=-1))
    c = jnp.dot(jnp.concatenate(stacked, axis=0) if n > 1 else stacked[0], suffix_mat,
                preferred_element_type=F32)
    return [(c[s * rows:(s + 1) * rows, :HEAD_DIM], c[s * rows:(s + 1) * rows, HEAD_DIM:])
            for s in range(n)]


def _sb_sums(z2, suffix_mat, visible=None, keep=None):
    log_beta, log_keep = _log_break(z2)
    if keep is not None:
        log_keep = log_keep * keep
    if visible is not None:
        log_keep = jnp.where(visible, log_keep, 0.0)
    return log_beta, _half_sums(log_keep, suffix_mat)


def _sb_weights(log_beta, halves, carry, visible=None, keep=None):
    after = []
    for suffix, total in reversed(halves):
        after.append(suffix + carry)
        carry = carry + total
    after = jnp.concatenate(after[::-1], axis=-1) if len(after) > 1 else after[0]
    w = jnp.exp2(log_beta + after)
    if keep is not None:
        w = w * keep
    if visible is not None:
        w = jnp.where(visible, w, 0.0)
    return w, carry


def _sb_block(z2, carry, suffix_mat, **mask):
    log_beta, halves = _sb_sums(z2, suffix_mat, **mask)
    return _sb_weights(log_beta, halves, carry, **mask)


SB_HEADS_PER_STEP = 4


def _sb_prompt_kernel(bias_ref, q_ref, k_ref, v_ref, g_ref, o_ref, *, tq, tk):
    hb = pl.program_id(1)
    i = pl.program_id(2)
    scale2 = HEAD_DIM ** -0.5 * LOG2_E
    suffix_mat = _suffix_matrix()
    n_diag = tq // tk
    heads = [slice(n * HEAD_DIM, (n + 1) * HEAD_DIM) for n in range(SB_HEADS_PER_STEP)]
    qs = [q_ref[:, sl] for sl in heads]
    bias2 = [bias_ref[hb * SB_HEADS_PER_STEP + n] * LOG2_E for n in range(SB_HEADS_PER_STEP)]

    def block(j, state, visible, first_row=0):
        start = pl.multiple_of(j * tk, tk)
        n_heads = len(heads)
        z, sums, new = [None] * n_heads, [None] * n_heads, [None] * n_heads
        for step in range(n_heads + 2):
            n = step
            if n < n_heads:
                kb = (k_ref[pl.ds(start, tk), heads[n]] * scale2).astype(BF16)
                z[n] = _dot_nt(qs[n][first_row:], kb) + bias2[n]
            n = step - 1
            if 0 <= n < n_heads:
                sums[n] = _sb_sums(z[n], suffix_mat, visible)
            n = step - 2
            if 0 <= n < n_heads:
                acc, carry = state[n]
                w, carry_new = _sb_weights(*sums[n], carry[first_row:], visible)
                vb = v_ref[pl.ds(start, tk), heads[n]].astype(BF16)
                acc_new = acc[first_row:] + jnp.dot(w.astype(BF16), vb, preferred_element_type=F32)
                if first_row:
                    acc_new = jnp.concatenate([acc[:first_row], acc_new], axis=0)
                    carry_new = jnp.concatenate([carry[:first_row], carry_new], axis=0)
                new[n] = (acc_new, carry_new)
        return tuple(new)

    row = lax.broadcasted_iota(jnp.int32, (tq, tk), 0)
    col = lax.broadcasted_iota(jnp.int32, (tq, tk), 1)
    zeros = jnp.zeros((tq, HEAD_DIM), F32)
    state = tuple((zeros, zeros) for _ in heads)
    for d in reversed(range(n_diag)):
        state = block(i * n_diag + d, state, (col + d * tk < row)[d * tk:], first_row=d * tk)

    n_before = i * n_diag
    state = lax.fori_loop(0, n_before, lambda jj, st: block(n_before - 1 - jj, st, None), state)
    for n, sl in enumerate(heads):
        o_ref[:, sl] = _rms_normalize(state[n][0], g_ref[:, sl]).astype(o_ref.dtype)


def _narrow_spec(rows, width, group, index_map):
    slot = INPROJ_BF16_GROUPS.index(group)
    return pl.BlockSpec((None, rows, width), lambda *idx: (slot,) + tuple(index_map(*idx)))


def _sb_prompt(narrow, k, v, bias, g, batch, seq, tq, tk):
    nq = seq // tq
    width = SB_HEADS_PER_STEP * HEAD_DIM
    assert tq % tk == 0 and tk % HEAD_DIM == 0 and N_HEADS % SB_HEADS_PER_STEP == 0
    return pl.pallas_call(
        functools.partial(_sb_prompt_kernel, tq=tq, tk=tk),
        grid=(batch, N_HEADS // SB_HEADS_PER_STEP, nq),
        in_specs=[
            pl.BlockSpec(memory_space=pltpu.SMEM),
            _narrow_spec(tq, width, 0, lambda b, h, i: (b * nq + i, h)),
            pl.BlockSpec((seq, width), lambda b, h, i: (b, h)),
            pl.BlockSpec((seq, width), lambda b, h, i: (b, h)),
            pl.BlockSpec((1, width), lambda b, h, i: (0, h)),
        ],
        out_specs=pl.BlockSpec((tq, width), lambda b, h, i: (b * nq + i, h)),
        out_shape=jax.ShapeDtypeStruct((batch * seq, GROUP_WIDTH), BF16),
        compiler_params=_params("parallel", "parallel", "arbitrary"),
        name="sb_prompt",
    )(bias, narrow, k, v, g)


def _sb_decode_kernel(pt_ref, q_ref, bias_ref, kn_ref, vn_ref, *refs, t_new, pages_per_step,
                      seqs_per_step):
    del pt_ref
    stages = _sb_decode_stages(pl.program_id(1), pl.num_programs(1), q_ref, bias_ref, kn_ref, vn_ref,
                               *refs, t_new=t_new, pages_per_step=pages_per_step,
                               seqs_per_step=seqs_per_step)
    for stage in stages:
        stage()


def _sb_decode_stages(p, n_steps, q_ref, bias_ref, kn_ref, vn_ref, *refs, t_new, pages_per_step,
                      seqs_per_step):
    n_page_refs = seqs_per_step * pages_per_step
    k_refs, v_refs = refs[:n_page_refs], refs[n_page_refs:2 * n_page_refs]
    g_ref, o_ref, acc_ref, carry_ref = refs[2 * n_page_refs:]
    rows = N_HEADS * t_new
    page_cols = k_refs[0].shape[2] * N_HEADS
    scale2 = HEAD_DIM ** -0.5 * LOG2_E
    suffix_mat = _suffix_matrix()
    seqs = range(seqs_per_step)

    def tiled(x, n):
        return jnp.concatenate([x] * n, axis=-1) if n > 1 else x

    def logits(keys):
        bias2 = tiled(bias_ref[...] * LOG2_E, keys[0].shape[0] // HEAD_DIM)
        return [_dot_nt(q_ref[s], keys[s].astype(BF16)) * scale2 + bias2 for s in seqs]

    def weigh(sums, values, acc, carry, **mask):
        out = []
        for s in seqs:
            w, new_carry = _sb_weights(*sums[s], carry[s], **mask)
            out.append((acc[s] + jnp.dot(w.astype(BF16), values[s].astype(BF16),
                                         preferred_element_type=F32), new_carry))
        return [o[0] for o in out], [o[1] for o in out]

    r = lax.broadcasted_iota(jnp.int32, (rows, HEAD_DIM), 0)
    c = lax.broadcasted_iota(jnp.int32, (rows, HEAD_DIM), 1)
    heads_agree = (c % N_HEADS) == (r // t_new)
    keep = tiled(jnp.where(heads_agree, 1.0, 0.0), page_cols * pages_per_step // HEAD_DIM)

    def pages(page_refs, s):
        mine = page_refs[s * pages_per_step:(s + 1) * pages_per_step]
        return jnp.concatenate([ref[0, 0].reshape(page_cols, HEAD_DIM) for ref in mine[::-1]], axis=0)

    live = {}

    def stage_logits():
        @pl.when(p == 0)
        def _():
            visible = heads_agree & (c // N_HEADS < r % t_new)
            pad = jnp.zeros((HEAD_DIM - rows, HEAD_DIM), F32)
            kn = [jnp.concatenate([kn_ref[s], pad], axis=0) for s in seqs]
            vn = [jnp.concatenate([vn_ref[s], pad], axis=0) for s in seqs]
            zeros = [jnp.zeros((rows, HEAD_DIM), F32)] * seqs_per_step
            sums = [_sb_sums(z, suffix_mat, visible=visible) for z in logits(kn)]
            acc, carry = weigh(sums, vn, zeros, zeros, visible=visible)
            for s in seqs:
                acc_ref[s] = acc[s]
                carry_ref[s] = carry[s]

        live["z2"] = logits([pages(k_refs, s) for s in seqs])

    def stage_sums():
        live["sums"] = [_sb_sums(z, suffix_mat, keep=keep) for z in live["z2"]]

    def stage_weights():
        acc, carry = weigh(live["sums"], [pages(v_refs, s) for s in seqs],
                           [acc_ref[s] for s in seqs], [carry_ref[s] for s in seqs], keep=keep)
        for s in seqs:
            acc_ref[s] = acc[s]
            carry_ref[s] = carry[s]

        @pl.when(p == n_steps - 1)
        def _():
            for s in seqs:
                o_ref[s] = _rms_normalize(acc[s], g_ref[...])

    return stage_logits, stage_sums, stage_weights


def _sb_decode(q_rows, bias_rows, k_new, v_new, cache_k, cache_v, page_table, g_rows, layer, t_new):
    plan = _sb_decode_plan(q_rows, bias_rows, k_new, v_new, cache_k, cache_v, page_table, g_rows,
                           layer, t_new, locate=lambda sb, p: (sb, p))
    return pl.pallas_call(
        functools.partial(_sb_decode_kernel, **plan["static"]),
        grid_spec=pltpu.PrefetchScalarGridSpec(
            num_scalar_prefetch=1,
            grid=plan["grid"],
            in_specs=plan["in_specs"],
            out_specs=plan["out_spec"],
            scratch_shapes=plan["scratch"],
        ),
        out_shape=plan["out_shape"],
        compiler_params=_params("parallel", "arbitrary"),
        name="sb_decode",
    )(page_table.reshape(-1), *plan["operands"])


def _sb_decode_plan(q_rows, bias_rows, k_new, v_new, cache_k, cache_v, page_table, g_rows, layer,
                    t_new, locate):
    n_seq, rows, _ = q_rows.shape
    n_pages = page_table.shape[1]
    page = cache_k.shape[2]
    assert cache_k.shape[3:] == (N_HEADS, HEAD_DIM) and rows <= HEAD_DIM
    pages_per_step = next(n for n in (4, 2, 1) if n_pages % n == 0)
    seqs_per_step = next(n for n in (2, 1) if n_seq % n == 0)
    row_spec = pl.BlockSpec((seqs_per_step, rows, HEAD_DIM),
                            lambda *idx: (locate(*idx[:-1])[0], 0, 0))
    const_spec = pl.BlockSpec((rows, HEAD_DIM), lambda *idx: (0, 0))

    def page_spec(s, i):
        def index_map(*idx):
            sb, p = locate(*idx[:-1])
            seq = sb * seqs_per_step + s
            return (layer, idx[-1][seq * n_pages + n_pages - 1 - (p * pages_per_step + i)], 0, 0, 0)
        return pl.BlockSpec((1, 1, page, N_HEADS, HEAD_DIM), index_map)

    page_specs = [page_spec(s, i) for s in range(seqs_per_step) for i in range(pages_per_step)]
    scratch = pltpu.VMEM((seqs_per_step, rows, HEAD_DIM), F32)
    return dict(
        grid=(n_seq // seqs_per_step, n_pages // pages_per_step),
        in_specs=[row_spec, const_spec, row_spec, row_spec] + page_specs + page_specs + [const_spec],
        operands=[q_rows, bias_rows, k_new, v_new] + [cache_k] * len(page_specs)
        + [cache_v] * len(page_specs) + [g_rows],
        out_spec=row_spec,
        out_shape=jax.ShapeDtypeStruct((n_seq, rows, HEAD_DIM), F32),
        scratch=[scratch, scratch],
        static=dict(t_new=t_new, pages_per_step=pages_per_step, seqs_per_step=seqs_per_step),
    )


def _lower_bound(logits, layer):
    e = jnp.exp(logits - jnp.max(logits, axis=0, keepdims=True))
    return jnp.sum(e[:layer + 1], axis=0, keepdims=True) / jnp.sum(e, axis=0, keepdims=True)


def _hgrn_gates(hq, hf, lb, log=jnp.log):
    f = lb + (1.0 - lb) * _sigmoid(hf)
    return _silu(hq), 1.0 - f, log(f)


def _block_tril(n, blk):
    r = lax.broadcasted_iota(jnp.int32, (n, n), 0)
    c = lax.broadcasted_iota(jnp.int32, (n, n), 1)
    return jnp.where((c <= r) & (r // blk == c // blk), 1.0, 0.0).astype(BF16)


HG_HEADS_PER_STEP = 4


def _hgrn_prompt_kernel(hq_ref, hf_ref, hi_ref, hg_ref, lbl_ref, g_ref, o_ref, s_ref, st_ref,
                        *, tc, layer):
    t = pl.program_id(2)
    n_chunks = tc // HG_CHUNK
    n_sub = HG_CHUNK // HG_SUB
    heads = [slice(n * HEAD_DIM, (n + 1) * HEAD_DIM) for n in range(HG_HEADS_PER_STEP)]
    every = range(len(heads))

    @pl.when(t == 0)
    def _():
        st_ref[...] = jnp.zeros_like(st_ref)

    tril_chunk = _block_tril(tc, HG_CHUNK)
    tril_sub = _block_tril(tc, HG_SUB)
    rr =lax.broadcasted_iota(jnp.int32, (2 * HEAD_DIM, 2 * HEAD_DIM), 0) // HEAD_DIM
    cc = lax.broadcasted_iota(jnp.int32, (2 * HEAD_DIM, 2 * HEAD_DIM), 1) // HEAD_DIM
    pair_ones = jnp.where(rr == cc, 1.0, 0.0).astype(BF16)
    sub_shape = (tc // HG_SUB, HG_SUB, HEAD_DIM)
    chunk_shape = (n_chunks, HG_CHUNK, HEAD_DIM)
    r_sub = lax.broadcasted_iota(jnp.int32, sub_shape, 1)
    r_chunk = lax.broadcasted_iota(jnp.int32, chunk_shape, 1)

    q, kk, g, v = [], [], [], []
    for sl in heads:
        lb = _lower_bound(lbl_ref[:, sl], layer)
        qh, kh, gh = _hgrn_gates(hq_ref[:, sl].astype(F32), hf_ref[:, sl], lb, jnp.log2)
        q.append(qh)
        kk.append(kh)
        g.append(gh)
        v.append(hi_ref[:, sl].astype(F32))
    v_bf = [x.astype(BF16) for x in v]

    b, c = [], []
    for n in every:
        g_hi, g_lo = _split_hi_lo(g[n])
        b.append(jnp.dot(tril_chunk, g_hi, preferred_element_type=F32)
                 + jnp.dot(tril_chunk, g_lo, preferred_element_type=F32))
        c.append(jnp.dot(tril_sub, g_hi, preferred_element_type=F32)
                 + jnp.dot(tril_sub, g_lo, preferred_element_type=F32))

    scores, shifted_v = [], []
    for n in every:
        q3, k3, c3, v3 = (a.reshape(sub_shape) for a in (q[n], kk[n], c[n], v[n]))
        prods, rolled = [q3 * k3], [v3]
        for delta in range(1, HG_SUB):
            ks = pltpu.roll(k3, delta, axis=1)
            cs = pltpu.roll(c3, delta, axis=1)
            prods.append(jnp.where(r_sub >= delta, q3 * ks * jnp.exp2(c3 - cs), 0.0))
            rolled.append(pltpu.roll(v3, delta, axis=1))
        prods = [x.reshape(tc, HEAD_DIM).astype(BF16) for x in prods]
        paired = [jnp.concatenate(prods[d:d + 2], axis=-1) for d in range(0, HG_SUB, 2)]
        scores.append(jnp.dot(jnp.concatenate(paired, axis=0), pair_ones,
                              preferred_element_type=F32))
        shifted_v.append(rolled)
    o = []
    for n in every:
        acc = jnp.zeros((tc, HEAD_DIM), F32)
        for delta in range(HG_SUB):
            rows = slice((delta // 2) * tc, (delta // 2 + 1) * tc)
            cols = slice((delta % 2) * HEAD_DIM, (delta % 2 + 1) * HEAD_DIM)
            acc = acc + scores[n][rows, cols] * shifted_v[n][delta].reshape(tc, HEAD_DIM)
        o.append(acc)

    for n in every:
        b3 = b[n].reshape(chunk_shape)
        kk3 = kk[n].reshape(chunk_shape)
        q_sub = (q[n] * jnp.exp2(c[n])).reshape(chunk_shape)
        q_cat, k_cat = [], []
        for i in range(1, n_sub):
            lo_row = i * HG_SUB
            b_start = b3[:, lo_row - 1:lo_row, :]
            k_cat.append(jnp.where(r_chunk < lo_row, kk3 * jnp.exp2(b_start - b3), 0.0).astype(BF16))
            q_cat.append(jnp.where((r_chunk >= lo_row) & (r_chunk < lo_row + HG_SUB),
                                   q_sub, 0.0).astype(BF16))
        a_off = jnp.einsum('ctk,csk->cts', jnp.concatenate(q_cat, axis=-1),
                           jnp.concatenate(k_cat, axis=-1), preferred_element_type=F32)
        o_off = jnp.einsum('cts,csv->ctv', a_off.astype(BF16), v_bf[n].reshape(chunk_shape),
                           preferred_element_type=F32)
        o[n] = o[n] + o_off.reshape(tc, HEAD_DIM)

    q_in = [(q[n] * jnp.exp2(b[n])).astype(BF16) for n in every]
    st = [st_ref[n] for n in every]
    o_inter = [[] for _ in every]
    for ci in range(n_chunks):
        rows = slice(ci * HG_CHUNK, (ci + 1) * HG_CHUNK)
        for n in every:
            b_c = b[n][rows]
            b_last = b_c[HG_CHUNK - 1:HG_CHUNK]
            o_inter[n].append(_dot_nt(q_in[n][rows], st[n].astype(BF16)))
            k_out = (kk[n][rows] * jnp.exp2(b_last - b_c)).astype(BF16)
            st[n] = st[n] * jnp.exp2(b_last) + _dot_tn(v_bf[n][rows], k_out)

    for n, sl in enumerate(heads):
        st_ref[n] = st[n]
        total = o[n] + jnp.concatenate(o_inter[n], axis=0)
        gate = _silu(hg_ref[:, sl].astype(F32))
        o_ref[:, sl] = (_rms_normalize(total, g_ref[:, sl]) * gate).astype(o_ref.dtype)

    @pl.when(t == pl.num_programs(2) - 1)
    def _():
        for n in every:
            s_ref[0, n] = st[n].T


def _hgrn_prompt(narrow, hf, lb_logits, g, batch, seq, tc, layer):
    nt = seq // tc
    n_layers = lb_logits.shape[0]
    width = HG_HEADS_PER_STEP * HEAD_DIM
    assert N_HEADS % HG_HEADS_PER_STEP == 0
    tok = lambda b, h, t: (b * nt + t, h)
    return pl.pallas_call(
        functools.partial(_hgrn_prompt_kernel, tc=tc, layer=layer),
        grid=(batch, N_HEADS // HG_HEADS_PER_STEP, nt),
        in_specs=[
            _narrow_spec(tc, width, 3, tok),
            pl.BlockSpec((tc, width), tok),
            _narrow_spec(tc, width, 5, tok),
            _narrow_spec(tc, width, 6, tok),
            pl.BlockSpec((n_layers, width), lambda b, h, t: (0, h)),
            pl.BlockSpec((1, width), lambda b, h, t: (0, h)),
        ],
        out_specs=[
            pl.BlockSpec((tc, width), tok),
            pl.BlockSpec((1, HG_HEADS_PER_STEP, HEAD_DIM, HEAD_DIM), lambda b, h, t: (b, h, 0, 0)),
        ],
        out_shape=[
            jax.ShapeDtypeStruct((batch * seq, GROUP_WIDTH), BF16),
            jax.ShapeDtypeStruct((batch, N_HEADS, HEAD_DIM, HEAD_DIM), F32),
        ],
        scratch_shapes=[pltpu.VMEM((HG_HEADS_PER_STEP, HEAD_DIM, HEAD_DIM), F32)],
        compiler_params=_params("parallel", "parallel", "arbitrary"),
        name="hgrn_prompt",
    )(narrow, hf, narrow, narrow, lb_logits, g)


def _hgrn_decode_kernel(hq_ref, hf_ref, hi_ref, hg_ref, s_ref, lbl_ref, g_ref, o_ref, so_ref,
                        *, t_new, layer, seqs_per_step):
    heads = [slice(h * HEAD_DIM, (h + 1) * HEAD_DIM) for h in range(N_HEADS)]
    pairs = [(t, u) for t in range(t_new) for u in range(t + 1)]
    pair_rows = -(-len(pairs) // 16) * 16
    ones = jnp.ones((HEAD_DIM, HEAD_DIM), BF16)
    lb = _lower_bound(lbl_ref[...], layer)
    for s in range(seqs_per_step):
        q, kk, g = _hgrn_gates(hq_ref[s].astype(F32), hf_ref[s], lb, jnp.log2)
        v = hi_ref[s].astype(F32)
        b_rows = [g[0:1]]
        for t in range(1, t_new):
            b_rows.append(b_rows[-1] + g[t:t + 1])
        b_last = b_rows[-1]
        b = jnp.concatenate(b_rows, axis=0)

        prods = [q[t:t + 1] * kk[u:u + 1] * jnp.exp2(b_rows[t] - b_rows[u]) for t, u in pairs]
        prods.append(jnp.zeros((pair_rows - len(pairs), GROUP_WIDTH), F32))
        prods = jnp.concatenate(prods, axis=0)
        a = jnp.dot(jnp.concatenate([prods[:, sl] for sl in heads], axis=0).astype(BF16), ones,
                    preferred_element_type=F32)

        q_in = (q * jnp.exp2(b)).astype(BF16)
        pad = jnp.zeros((HG_SUB - t_new, GROUP_WIDTH), F32)
        k_out = jnp.concatenate([kk * jnp.exp2(b_last - b), pad], axis=0).astype(BF16)
        v_pad = jnp.concatenate([v, pad], axis=0).astype(BF16)
        decay_t = jnp.concatenate([jnp.exp2(b_last[:, sl]) for sl in heads], axis=0).T
        gate = _silu(hg_ref[s].astype(F32))
        for h, sl in enumerate(heads):
            state = s_ref[0, s, h]
            o = jnp.dot(q_in[:, sl], state.astype(BF16), preferred_element_type=F32)
            rows = []
            for t in range(t_new):
                row = o[t:t + 1]
                for u in range(t + 1):
                    i = h * pair_rows + pairs.index((t, u))
                    row = row + a[i:i + 1] * v[u:u + 1, sl]
                rows.append(row)
            o = jnp.concatenate(rows, axis=0)
            o_ref[s, :, sl] = _rms_normalize(o, g_ref[:, sl]) * gate[:, sl]
            so_ref[0, s, h] = state * decay_t[:, h:h + 1] + _dot_tn(k_out[:, sl], v_pad[:, sl])


def _hgrn_decode(hq, hf, hi, hg, state, lb_logits, g, layer):
    n_seq, t_new, _ = hq.shape
    assert t_new <= HG_SUB
    n_layers = lb_logits.shape[0]
    seqs_per_step = next(n for n in (2, 1) if n_seq % n == 0)
    row_spec = pl.BlockSpec((seqs_per_step, t_new, GROUP_WIDTH), lambda s: (s, 0, 0))
    state_block = (1, seqs_per_step, N_HEADS, HEAD_DIM, HEAD_DIM)
    return pl.pallas_call(
        functools.partial(_hgrn_decode_kernel, t_new=t_new, layer=layer,
                          seqs_per_step=seqs_per_step),
        grid=(n_seq // seqs_per_step,),
        in_specs=[
            row_spec, row_spec, row_spec, row_spec,
            pl.BlockSpec(state_block, lambda s: (layer, s, 0, 0, 0)),
            pl.BlockSpec((n_layers, GROUP_WIDTH), lambda s: (0, 0)),
            pl.BlockSpec((1, GROUP_WIDTH), lambda s: (0, 0)),
        ],
        out_specs=[
            row_spec,
            pl.BlockSpec(state_block, lambda s: (0, s, 0, 0, 0)),
        ],
        out_shape=[
            jax.ShapeDtypeStruct((n_seq, t_new, GROUP_WIDTH), F32),
            jax.ShapeDtypeStruct((1, n_seq, N_HEADS, HEAD_DIM, HEAD_DIM), F32),
        ],
        compiler_params=_params("parallel"),
        name="hgrn_decode",
    )(hq, hf, hi, hg, state, lb_logits, g)


def _outproj_kernel(sb_ref, hg_ref, w_ref, x_ref, o_ref):
    half = sb_ref.shape[1]
    w = w_ref[...].astype(BF16)
    acc = jnp.dot(sb_ref[...].astype(BF16), w[:half], preferred_element_type=F32)
    acc = acc + jnp.dot(hg_ref[...].astype(BF16), w[half:], preferred_element_type=F32)
    o_ref[...] = x_ref[...] + acc


def _outproj(sb, hg, w, x, tm, tn):
    m, d = x.shape
    width = sb.shape[1]
    return pl.pallas_call(
        _outproj_kernel,
        grid=(m // tm, d // tn),
        in_specs=[
            pl.BlockSpec((tm, width), lambda i, j: (i, 0)),
            pl.BlockSpec((tm, width), lambda i, j: (i, 0)),
            pl.BlockSpec((2 * width, tn), lambda i, j: (0, j)),
            pl.BlockSpec((tm, tn), lambda i, j: (i, j)),
        ],
        out_specs=pl.BlockSpec((tm, tn), lambda i, j: (i, j)),
        out_shape=jax.ShapeDtypeStruct((m, d), F32),
        compiler_params=_params("parallel", "arbitrary"),
        name="outproj",
    )(sb, hg, w, x)


def _mlp_kernel(h_ref, g2_ref, wu_ref, wd_ref, gf_ref, y_ref, hn_ref, *, final_norm):
    j = pl.program_id(1)

    @pl.when(j == 0)
    def _():
        hn_ref[...] = _rms_normalize(h_ref[...], g2_ref[...]).astype(BF16)
        y_ref[...] = jnp.zeros_like(y_ref)

    u = jnp.dot(hn_ref[...], wu_ref[...], preferred_element_type=F32)
    a = jnp.square(jnp.maximum(u, 0.0)).astype(BF16)
    y_ref[...] += jnp.dot(a, wd_ref[...], preferred_element_type=F32)

    @pl.when(j == pl.num_programs(1) - 1)
    def _():
        y = h_ref[...] + y_ref[...]
        y_ref[...] = _rms_normalize(y, gf_ref[...]) if final_norm else y


def _mlp(h, g2, wu_bf16, wd_bf16, gf, tm, tf, final_norm):
    m, d = h.shape
    f = wu_bf16.shape[1]
    return pl.pallas_call(
        functools.partial(_mlp_kernel, final_norm=final_norm),
        grid=(m // tm, f // tf),
        in_specs=[
            pl.BlockSpec((tm, d), lambda i, j: (i, 0)),
            pl.BlockSpec((1, d), lambda i, j: (0, 0)),
            pl.BlockSpec((d, tf), lambda i, j: (0, j)),
            pl.BlockSpec((tf, d), lambda i, j: (j, 0)),
            pl.BlockSpec((1, d), lambda i, j: (0, 0)),
        ],
        out_specs=pl.BlockSpec((tm, d), lambda i, j: (i, 0)),
        out_shape=jax.ShapeDtypeStruct((m, d), F32),
        scratch_shapes=[pltpu.VMEM((tm, d), BF16)],
        compiler_params=_params("parallel", "arbitrary"),
        name="mlp",
    )(h, g2, wu_bf16, wd_bf16, gf)


def _mlp_decode_kernel(pt_ref, h_ref, g2_ref, wu_ref, wd_ref, gf_ref, *refs, final_norm,
                       page_steps, **decode_static):
    del pt_ref
    decode_refs, y_ref, sb_ref = refs[:-5], refs[-5], refs[-4]
    hn_ref, dec_acc_ref, dec_carry_ref = refs[-3:]
    j = pl.program_id(1)
    step = pl.program_id(0) * pl.num_programs(1) + j
    logits, sums, weights = _sb_decode_stages(step % page_steps, page_steps, *decode_refs, sb_ref,
                                              dec_acc_ref, dec_carry_ref, **decode_static)

    @pl.when(j == 0)
    def _():
        hn_ref[...] = _rms_normalize(h_ref[...], g2_ref[...]).astype(BF16)
        y_ref[...] = jnp.zeros_like(y_ref)

    logits()
    u = jnp.dot(hn_ref[...], wu_ref[...], preferred_element_type=F32)
    sums()
    a = jnp.square(jnp.maximum(u, 0.0)).astype(BF16)
    y_ref[...] += jnp.dot(a, wd_ref[...], preferred_element_type=F32)
    weights()

    @pl.when(j == pl.num_programs(1) - 1)
    def _():
        y = h_ref[...] + y_ref[...]
        y_ref[...] = _rms_normalize(y, gf_ref[...]) if final_norm else y


def _mlp_decode_fits(m, f, tm, tf, n_seq, n_pages):
    pages_per_step = next(n for n in (4, 2, 1) if n_pages % n == 0)
    seqs_per_step = next(n for n in (2, 1) if n_seq % n == 0)
    return (m // tm) * (f // tf) == (n_seq // seqs_per_step) * (n_pages // pages_per_step)


def _mlp_decode(h, g2, wu_bf16, wd_bf16, gf, tm, tf, final_norm, *decode_args):
    m, d = h.shape
    f = wu_bf16.shape[1]
    nj = f // tf
    page_table = decode_args[6]
    n_pages = page_table.shape[1]

    def locate(i, j):
        step = i * nj + j
        return step // page_steps, step % page_steps

    plan = _sb_decode_plan(*decode_args, locate=locate)
    page_steps = plan["grid"][1]
    assert (m // tm) * nj == plan["grid"][0] * page_steps
    return pl.pallas_call(
        functools.partial(_mlp_decode_kernel, final_norm=final_norm, page_steps=page_steps,
                          **plan["static"]),
        grid_spec=pltpu.PrefetchScalarGridSpec(
            num_scalar_prefetch=1,
            grid=(m // tm, nj),
            in_specs=[
                pl.BlockSpec((tm, d), lambda i, j, pt: (i, 0)),
                pl.BlockSpec((1, d), lambda i, j, pt: (0, 0)),
                pl.BlockSpec((d, tf), lambda i, j, pt: (0, j)),
                pl.BlockSpec((tf, d), lambda i, j, pt: (j, 0)),
                pl.BlockSpec((1, d), lambda i, j, pt: (0, 0)),
            ] + plan["in_specs"],
            out_specs=[pl.BlockSpec((tm, d), lambda i, j, pt: (i, 0)), plan["out_spec"]],
            scratch_shapes=[pltpu.VMEM((tm, d), BF16)] + plan["scratch"],
        ),
        out_shape=[jax.ShapeDtypeStruct((m, d), F32), plan["out_shape"]],
        compiler_params=pltpu.CompilerParams(dimension_semantics=("arbitrary", "arbitrary"),
                                             vmem_limit_bytes=VMEM_LIMIT_FUSED),
        name="mlp_decode",
    )(page_table.reshape(-1), h, g2, wu_bf16, wd_bf16, gf, *plan["operands"])


def _row_tile(m, target):
    return target if m % target == 0 else m


def kernel(x_prompt, x_sample, cache_k, cache_v, state_hgrn, page_table, norm1_g, w_in, sb_bias,
           sb_norm_g, hg_norm_g, hg_lb_logits, w_out, norm2_g, w_up, w_down, final_norm_g):
    batch, seq, d = x_prompt.shape
    n_seq, t_new, _ = x_sample.shape
    depth = w_in.shape[0]
    hp = x_prompt.reshape(batch * seq, d)
    hs = x_sample.reshape(n_seq * t_new, d)
    tm_in_p = _row_tile(batch * seq, 1024)
    tm_out_p = _row_tile(batch * seq, 1024)
    tm_p = _row_tile(batch * seq, 512)
    tm_s = _row_tile(n_seq * t_new, 512)
    tn_in = 512
    tf = 1024
    tq =_row_tile(seq, 512)
    tk_sb = _row_tile(tq, 256)
    tc = _row_tile(seq, 256)
    gf = final_norm_g.reshape(1, d)

    outs = {k: [] for k in ("kp", "vp", "sp", "ks", "vs", "ss")}
    for l in range(depth):
        g1 = norm1_g[l].reshape(1, d)
        g2 = norm2_g[l].reshape(1, d)
        sbg = sb_norm_g[l].reshape(1, GROUP_WIDTH)
        hgg = hg_norm_g[l].reshape(1, GROUP_WIDTH)
        w_in_l = w_in[l]
        w_out_l = w_out[l]
        w_up_l = w_up[l].astype(BF16)
        w_down_l = w_down[l].astype(BF16)
        last = l == depth - 1

        sk, sv, hf, narrow = _inproj(hp, g1, w_in_l, tm_in_p, tn_in)
        sb = _sb_prompt(narrow, sk, sv, sb_bias[l], sbg, batch, seq, tq, tk_sb)
        hgo, s_p = _hgrn_prompt(narrow, hf, hg_lb_logits, hgg, batch, seq, tc, l)
        hmid = _outproj(sb, hgo, w_out_l, hp, tm_out_p, 1024)
        outs["kp"].append(sk.reshape(batch, seq, N_HEADS, HEAD_DIM))
        outs["vp"].append(sv.reshape(batch, seq, N_HEADS, HEAD_DIM))
        outs["sp"].append(s_p)

        *wide_s, narrow_s = _inproj(hs, g1, w_in_l, tm_s, tn_in)
        tk, tv, uf = (a.reshape(n_seq, t_new, GROUP_WIDTH) for a in wide_s)
        tq_, uq, ui, ug = (narrow_s[n].reshape(n_seq, t_new, GROUP_WIDTH) for n in range(4))
        rows = N_HEADS * t_new
        sbg_rows = jnp.repeat(sbg.reshape(N_HEADS, HEAD_DIM), t_new, axis=0)
        bias_rows = jnp.broadcast_to(jnp.repeat(sb_bias[l], t_new)[:, None], (rows, HEAD_DIM))
        q_rows = tq_.reshape(n_seq, t_new, N_HEADS, HEAD_DIM).transpose(0, 2, 1, 3)
        decode_args = (q_rows.reshape(n_seq, rows, HEAD_DIM), bias_rows,
                       tk.reshape(n_seq, rows, HEAD_DIM), tv.reshape(n_seq, rows, HEAD_DIM),
                       cache_k, cache_v, page_table, sbg_rows, l, t_new)
        if _mlp_decode_fits(batch * seq, w_up_l.shape[1], tm_p, tf, n_seq, page_table.shape[1]):
            hp, sb2 = _mlp_decode(hmid, g2, w_up_l, w_down_l, gf, tm_p, tf, last, *decode_args)
        else:
            hp = _mlp(hmid, g2, w_up_l, w_down_l, gf, tm_p, tf, last)
            sb2 = _sb_decode(*decode_args)
        sb2 = sb2.reshape(n_seq, N_HEADS, t_new, HEAD_DIM).transpose(0, 2, 1, 3)
        sb2 = sb2.reshape(n_seq * t_new, GROUP_WIDTH)
        hgo2, s_s = _hgrn_decode(uq, uf, ui, ug, state_hgrn, hg_lb_logits, hgg, l)
        hmid2 = _outproj(sb2, hgo2.reshape(n_seq * t_new, GROUP_WIDTH), w_out_l, hs, tm_s, 1024)
        hs = _mlp(hmid2, g2, w_up_l, w_down_l, gf, tm_s, tf, last)
        outs["ks"].append(tk.reshape(n_seq, t_new, N_HEADS, HEAD_DIM))
        outs["vs"].append(tv.reshape(n_seq, t_new, N_HEADS, HEAD_DIM))
        outs["ss"].append(s_s[0])

    y_prompt = hp.reshape(batch, seq, d)
    y_sample = hs.reshape(n_seq, t_new, d)
    return (y_prompt, y_sample, jnp.stack(outs["kp"]), jnp.stack(outs["vp"]), jnp.stack(outs["sp"]),
            jnp.stack(outs["ks"]), jnp.stack(outs["vs"]), jnp.stack(outs["ss"]))
```

```python
import functools

import jax
import jax.numpy as jnp
from jax import lax
from jax.experimental import pallas as pl
from jax.experimental.pallas import tpu as pltpu

F32 = jnp.float32
BF16 = jnp.bfloat16

RMS_EPS = 1e-6
HEAD_DIM = 128
N_HEADS = 8
GROUP_WIDTH = N_HEADS * HEAD_DIM
HG_CHUNK = 32
HG_SUB = 8
VMEM_LIMIT = 56 * 1024 * 1024
VMEM_LIMIT_FUSED = 60 * 1024 * 1024


def _params(*sem):
    return pltpu.CompilerParams(dimension_semantics=sem, vmem_limit_bytes=VMEM_LIMIT)


def _rms_normalize(x, g):
    return x * lax.rsqrt(jnp.mean(x * x, axis=-1, keepdims=True) + RMS_EPS) * g


def _split_hi_lo(x):
    hi = x.astype(BF16)
    lo = (x - hi.astype(F32)).astype(BF16)
    return hi, lo


def _dot_nt(a, b):
    return lax.dot_general(a, b, (((1,), (1,)), ((), ())), preferred_element_type=F32)


def _dot_tn(a, b):
    return lax.dot_general(a, b, (((0,), (0,)), ((), ())), preferred_element_type=F32)


def _sigmoid(x):
    return 1.0 / (1.0 + jnp.exp(-x))


def _silu(x):
    return x * _sigmoid(x)


INPROJ_F32_GROUPS = (1, 2, 4)
INPROJ_BF16_GROUPS = (0, 3, 5, 6)


def _inproj_kernel(order_ref, x_ref, g_ref, w_ref, *refs, parts):
    del order_ref
    f32_refs, narrow_ref, xn_ref = refs[:-2], refs[-2], refs[-1]
    j = pl.program_id(1)

    @pl.when(j == 0)
    def _():
        xn_ref[...] = _rms_normalize(x_ref[...], g_ref[...]).astype(BF16)

    for n, o_ref in enumerate(f32_refs):
        @pl.when(j // parts == n)
        def _(o_ref=o_ref):
            o_ref[...] = jnp.dot(xn_ref[...], w_ref[...], preferred_element_type=F32)

    @pl.when(j >= len(f32_refs) * parts)
    def _():
        narrow_ref[...] = jnp.dot(xn_ref[...], w_ref[...],
                                  preferred_element_type=F32).astype(BF16)


def _inproj(x, g, w, tm, tn):
    m, d = x.shape
    n_wide, n_narrow = len(INPROJ_F32_GROUPS), len(INPROJ_BF16_GROUPS)
    assert w.shape[1] == (n_wide + n_narrow) * GROUP_WIDTH and m % tm == 0
    parts = GROUP_WIDTH // tn
    order = jnp.array(INPROJ_F32_GROUPS + INPROJ_BF16_GROUPS, jnp.int32)

    def f32_spec(n):
        return pl.BlockSpec((tm, tn), lambda i, j, order: (i, jnp.clip(j - n * parts, 0, parts - 1)))

    def narrow_map(i, j, order):
        jn = jnp.maximum(j - n_wide * parts, 0)
        return (jn // parts, i, jn % parts)

    return pl.pallas_call(
        functools.partial(_inproj_kernel, parts=parts),
        grid_spec=pltpu.PrefetchScalarGridSpec(
            num_scalar_prefetch=1,
            grid=(m // tm, (n_wide + n_narrow) * parts),
            in_specs=[
                pl.BlockSpec((tm, d), lambda i, j, order: (i, 0)),
                pl.BlockSpec((1, d), lambda i, j, order: (0, 0)),
                pl.BlockSpec((d, tn), lambda i, j, order: (0, order[j // parts] * parts + j % parts)),
            ],
            out_specs=[f32_spec(n) for n in range(n_wide)]
            + [pl.BlockSpec((None, tm, tn), narrow_map)],
            scratch_shapes=[pltpu.VMEM((tm, d), BF16)],
        ),
        out_shape=[jax.ShapeDtypeStruct((m, GROUP_WIDTH), F32)] * n_wide
        + [jax.ShapeDtypeStruct((n_narrow, m, GROUP_WIDTH), BF16)],
        compiler_params=_params("parallel", "arbitrary"),
        name="inproj",
    )(order, x, g, w)


SB_SUM_PARTS = 1


def _suffix_matrix():
    shape = (SB_SUM_PARTS * HEAD_DIM, 2 * HEAD_DIM)
    r = lax.broadcasted_iota(jnp.int32, shape, 0) % HEAD_DIM
    c = lax.broadcasted_iota(jnp.int32, shape, 1)
    return jnp.where((c >= HEAD_DIM) | (r > c), 1.0, 0.0).astype(BF16)


LOG2_E = 1.4426950408889634


def _log_break(z2):
    l = jnp.log2(1.0 + jnp.exp2(-jnp.abs(z2)))
    log_beta = jnp.minimum(z2, 0.0) - l
    return log_beta, log_beta - z2


def _half_sums(log_keep, suffix_mat):
    parts = _split_hi_lo(log_keep) if suffix_mat.shape[0] == 2 * HEAD_DIM else (log_keep.astype(BF16),)
    rows, n = log_keep.shape[0], log_keep.shape[1] // HEAD_DIM
    stacked = []
    for s in range(n):
        sl = slice(s * HEAD_DIM, (s + 1) * HEAD_DIM)
        stacked.append(jnp.concatenate([p[:, sl] for p in parts], axis=-1))
    c = jnp.dot(jnp.concatenate(stacked, axis=0) if n > 1 else stacked[0], suffix_mat,
                preferred_element_type=F32)
    return [(c[s * rows:(s + 1) * rows, :HEAD_DIM], c[s * rows:(s + 1) * rows, HEAD_DIM:])
            for s in range(n)]


def _sb_sums(z2, suffix_mat, visible=None, keep=None):
    log_beta, log_keep = _log_break(z2)
    if keep is not None:
        log_keep = log_keep * keep
    if visible is not None:
        log_keep = jnp.where(visible, log_keep, 0.0)
    return log_beta, _half_sums(log_keep, suffix_mat)


def _sb_weights(log_beta, halves, carry, visible=None, keep=None):
    after = []
    for suffix, total in reversed(halves):
        after.append(suffix + carry)
        carry = carry + total
    after = jnp.concatenate(after[::-1], axis=-1) if len(after) > 1 else after[0]
    w = jnp.exp2(log_beta + after)
    if keep is not None:
        w = w * keep
    if visible is not None:
        w = jnp.where(visible, w, 0.0)
    return w, carry


def _sb_block(z2, carry, suffix_mat, **mask):
    log_beta, halves = _sb_sums(z2, suffix_mat, **mask)
    return _sb_weights(log_beta, halves, carry, **mask)


SB_HEADS_PER_STEP = 4


def _sb_prompt_kernel(bias_ref, q_ref, k_ref, v_ref, g_ref, o_ref, *, tq, tk):
    hb = pl.program_id(1)
    i = pl.program_id(2)
    scale2 = HEAD_DIM ** -0.5 * LOG2_E
    suffix_mat = _suffix_matrix()
    n_diag = tq // tk
    heads = [slice(n * HEAD_DIM, (n + 1) * HEAD_DIM) for n in range(SB_HEADS_PER_STEP)]
    qs = [q_ref[:, sl] for sl in heads]
    bias2 = [bias_ref[hb * SB_HEADS_PER_STEP + n] * LOG2_E for n in range(SB_HEADS_PER_STEP)]

    def block(j, state, visible, first_row=0):
        start = pl.multiple_of(j * tk, tk)
        n_heads = len(heads)
        z, sums, new = [None] * n_heads, [None] * n_heads, [None] * n_heads
        for step in range(n_heads + 2):
            n = step
            if n < n_heads:
                kb = (k_ref[pl.ds(start, tk), heads[n]] * scale2).astype(BF16)
                z[n] = _dot_nt(qs[n][first_row:], kb) + bias2[n]
            n = step - 1
            if 0 <= n < n_heads:
                sums[n] = _sb_sums(z[n], suffix_mat, visible)
            n = step - 2
            if 0 <= n < n_heads:
                acc, carry = state[n]
                w, carry_new = _sb_weights(*sums[n], carry[first_row:], visible)
                vb = v_ref[pl.ds(start, tk), heads[n]].astype(BF16)
                acc_new = acc[first_row:] + jnp.dot(w.astype(BF16), vb, preferred_element_type=F32)
                if first_row:
                    acc_new = jnp.concatenate([acc[:first_row], acc_new], axis=0)
                    carry_new = jnp.concatenate([carry[:first_row], carry_new], axis=0)
                new[n] = (acc_new, carry_new)
        return tuple(new)

    row = lax.broadcasted_iota(jnp.int32, (tq, tk), 0)
    col = lax.broadcasted_iota(jnp.int32, (tq, tk), 1)
    zeros = jnp.zeros((tq, HEAD_DIM), F32)
    state = tuple((zeros, zeros) for _ in heads)
    for d in reversed(range(n_diag)):
        state = block(i * n_diag + d, state, (col + d * tk < row)[d * tk:], first_row=d * tk)

    n_before = i * n_diag
    state = lax.fori_loop(0, n_before, lambda jj, st: block(n_before - 1 - jj, st, None), state)
    for n, sl in enumerate(heads):
        o_ref[:, sl] = _rms_normalize(state[n][0], g_ref[:, sl]).astype(o_ref.dtype)


def _narrow_spec(rows, width, group, index_map):
    slot = INPROJ_BF16_GROUPS.index(group)
    return pl.BlockSpec((None, rows, width), lambda *idx: (slot,) + tuple(index_map(*idx)))


def _sb_prompt(narrow, k, v, bias, g, batch, seq, tq, tk):
    nq = seq // tq
    width = SB_HEADS_PER_STEP * HEAD_DIM
    assert tq % tk == 0 and tk % HEAD_DIM == 0 and N_HEADS % SB_HEADS_PER_STEP == 0
    return pl.pallas_call(
        functools.partial(_sb_prompt_kernel, tq=tq, tk=tk),
        grid=(batch, N_HEADS // SB_HEADS_PER_STEP, nq),
        in_specs=[
            pl.BlockSpec(memory_space=pltpu.SMEM),
            _narrow_spec(tq, width, 0, lambda b, h, i: (b * nq + i, h)),
            pl.BlockSpec((seq, width), lambda b, h, i: (b, h)),
            pl.BlockSpec((seq, width), lambda b, h, i: (b, h)),
            pl.BlockSpec((1, width), lambda b, h, i: (0, h)),
        ],
        out_specs=pl.BlockSpec((tq, width), lambda b, h, i: (b * nq + i, h)),
        out_shape=jax.ShapeDtypeStruct((batch * seq, GROUP_WIDTH), BF16),
        compiler_params=_params("parallel", "parallel", "arbitrary"),
        name="sb_prompt",
    )(bias, narrow, k, v, g)


def _sb_decode_kernel(pt_ref, q_ref, bias_ref, kn_ref, vn_ref, *refs, t_new, pages_per_step,
                      seqs_per_step):
    del pt_ref
    stages = _sb_decode_stages(pl.program_id(1), pl.num_programs(1), q_ref, bias_ref, kn_ref, vn_ref,
                               *refs, t_new=t_new, pages_per_step=pages_per_step,
                               seqs_per_step=seqs_per_step)
    for stage in stages:
        stage()


def _sb_decode_stages(p, n_steps, q_ref, bias_ref, kn_ref, vn_ref, *refs, t_new, pages_per_step,
                      seqs_per_step):
    n_page_refs = seqs_per_step * pages_per_step
    k_refs, v_refs = refs[:n_page_refs], refs[n_page_refs:2 * n_page_refs]
    g_ref, o_ref, acc_ref, carry_ref = refs[2 * n_page_refs:]
    rows = N_HEADS * t_new
    page_cols = k_refs[0].shape[2] * N_HEADS
    scale2 = HEAD_DIM ** -0.5 * LOG2_E
    suffix_mat = _suffix_matrix()
    seqs = range(seqs_per_step)

    def tiled(x, n):
        return jnp.concatenate([x] * n, axis=-1) if n > 1 else x

    def logits(keys):
        bias2 = tiled(bias_ref[...] * LOG2_E, keys[0].shape[0] // HEAD_DIM)
        return [_dot_nt(q_ref[s], keys[s].astype(BF16)) * scale2 + bias2 for s in seqs]

    def weigh(sums, values, acc, carry, **mask):
        out = []
        for s in seqs:
            w, new_carry = _sb_weights(*sums[s], carry[s], **mask)
            out.append((acc[s] + jnp.dot(w.astype(BF16), values[s].astype(BF16),
                                         preferred_element_type=F32), new_carry))
        return [o[0] for o in out], [o[1] for o in out]

    r = lax.broadcasted_iota(jnp.int32, (rows, HEAD_DIM), 0)
    c = lax.broadcasted_iota(jnp.int32, (rows, HEAD_DIM), 1)
    heads_agree = (c % N_HEADS) == (r // t_new)
    keep = tiled(jnp.where(heads_agree, 1.0, 0.0), page_cols * pages_per_step // HEAD_DIM)

    def pages(page_refs, s):
        mine = page_refs[s * pages_per_step:(s + 1) * pages_per_step]
        return jnp.concatenate([ref[0, 0].reshape(page_cols, HEAD_DIM) for ref in mine[::-1]], axis=0)

    live = {}

    def stage_logits():
        @pl.when(p == 0)
        def _():
            visible = heads_agree & (c // N_HEADS < r % t_new)
            pad = jnp.zeros((HEAD_DIM - rows, HEAD_DIM), F32)
            kn = [jnp.concatenate([kn_ref[s], pad], axis=0) for s in seqs]
            vn = [jnp.concatenate([vn_ref[s], pad], axis=0) for s in seqs]
            zeros = [jnp.zeros((rows, HEAD_DIM), F32)] * seqs_per_step
            sums = [_sb_sums(z, suffix_mat, visible=visible) for z in logits(kn)]
            acc, carry = weigh(sums, vn, zeros, zeros, visible=visible)
            for s in seqs:
                acc_ref[s] = acc[s]
                carry_ref[s] = carry[s]

        live["z2"] = logits([pages(k_refs, s) for s in seqs])

    def stage_sums():
        live["sums"] = [_sb_sums(z, suffix_mat, keep=keep) for z in live["z2"]]

    def stage_weights():
        acc, carry = weigh(live["sums"], [pages(v_refs, s) for s in seqs],
                           [acc_ref[s] for s in seqs], [carry_ref[s] for s in seqs], keep=keep)
        for s in seqs:
            acc_ref[s] = acc[s]
            carry_ref[s] = carry[s]

        @pl.when(p == n_steps - 1)
        def _():
            for s in seqs:
                o_ref[s] = _rms_normalize(acc[s], g_ref[...])

    return stage_logits, stage_sums, stage_weights


def _sb_decode(q_rows, bias_rows, k_new, v_new, cache_k, cache_v, page_table, g_rows, layer, t_new):
    plan = _sb_decode_plan(q_rows, bias_rows, k_new, v_new, cache_k, cache_v, page_table, g_rows,
                           layer, t_new, locate=lambda sb, p: (sb, p))
    return pl.pallas_call(
        functools.partial(_sb_decode_kernel, **plan["static"]),
        grid_spec=pltpu.PrefetchScalarGridSpec(
            num_scalar_prefetch=1,
            grid=plan["grid"],
            in_specs=plan["in_specs"],
            out_specs=plan["out_spec"],
            scratch_shapes=plan["scratch"],
        ),
        out_shape=plan["out_shape"],
        compiler_params=_params("parallel", "arbitrary"),
        name="sb_decode",
    )(page_table.reshape(-1), *plan["operands"])


def _sb_decode_plan(q_rows, bias_rows, k_new, v_new, cache_k, cache_v, page_table, g_rows, layer,
                    t_new, locate):
    n_seq, rows, _ = q_rows.shape
    n_pages = page_table.shape[1]
    page = cache_k.shape[2]
    assert cache_k.shape[3:] == (N_HEADS, HEAD_DIM) and rows <= HEAD_DIM
    pages_per_step = next(n for n in (4, 2, 1) if n_pages % n == 0)
    seqs_per_step = next(n for n in (2, 1) if n_seq % n == 0)
    row_spec = pl.BlockSpec((seqs_per_step, rows, HEAD_DIM),
                            lambda *idx: (locate(*idx[:-1])[0], 0, 0))
    const_spec = pl.BlockSpec((rows, HEAD_DIM), lambda *idx: (0, 0))

    def page_spec(s, i):
        def index_map(*idx):
            sb, p = locate(*idx[:-1])
            seq = sb * seqs_per_step + s
            return (layer, idx[-1][seq * n_pages + n_pages - 1 - (p * pages_per_step + i)], 0, 0, 0)
        return pl.BlockSpec((1, 1, page, N_HEADS, HEAD_DIM), index_map)

    page_specs = [page_spec(s, i) for s in range(seqs_per_step) for i in range(pages_per_step)]
    scratch = pltpu.VMEM((seqs_per_step, rows, HEAD_DIM), F32)
    return dict(
        grid=(n_seq // seqs_per_step, n_pages // pages_per_step),
        in_specs=[row_spec, const_spec, row_spec, row_spec] + page_specs + page_specs + [const_spec],
        operands=[q_rows, bias_rows, k_new, v_new] + [cache_k] * len(page_specs)
        + [cache_v] * len(page_specs) + [g_rows],
        out_spec=row_spec,
        out_shape=jax.ShapeDtypeStruct((n_seq, rows, HEAD_DIM), F32),
        scratch=[scratch, scratch],
        static=dict(t_new=t_new, pages_per_step=pages_per_step, seqs_per_step=seqs_per_step),
    )


def _lower_bound(logits, layer):
    e = jnp.exp(logits - jnp.max(logits, axis=0, keepdims=True))
    return jnp.sum(e[:layer + 1], axis=0, keepdims=True) / jnp.sum(e, axis=0, keepdims=True)


def _hgrn_gates(hq, hf, lb, log=jnp.log):
    f = lb + (1.0 - lb) * _sigmoid(hf)
    return _silu(hq), 1.0 - f, log(f)


def _block_tril(n, blk):
    r = lax.broadcasted_iota(jnp.int32, (n, n), 0)
    c = lax.broadcasted_iota(jnp.int32, (n, n), 1)
    return jnp.where((c <= r) & (r // blk == c // blk), 1.0, 0.0).astype(BF16)


HG_HEADS_PER_STEP = 4


def _hgrn_prompt_kernel(hq_ref, hf_ref, hi_ref, hg_ref, lbl_ref, g_ref, o_ref, s_ref, st_ref,
                        *, tc, layer):
    t = pl.program_id(2)
    n_chunks = tc // HG_CHUNK
    n_sub = HG_CHUNK // HG_SUB
    heads = [slice(n * HEAD_DIM, (n + 1) * HEAD_DIM) for n in range(HG_HEADS_PER_STEP)]
    every = range(len(heads))

    @pl.when(t == 0)
    def _():
        st_ref[...] = jnp.zeros_like(st_ref)

    tril_chunk = _block_tril(tc, HG_CHUNK)
    tril_sub = _block_tril(tc, HG_SUB)
    rr =lax.broadcasted_iota(jnp.int32, (2 * HEAD_DIM, 2 * HEAD_DIM), 0) // HEAD_DIM
    cc = lax.broadcasted_iota(jnp.int32, (2 * HEAD_DIM, 2 * HEAD_DIM), 1) // HEAD_DIM
    pair_ones = jnp.where(rr == cc, 1.0, 0.0).astype(BF16)
    sub_shape = (tc // HG_SUB, HG_SUB, HEAD_DIM)
    chunk_shape = (n_chunks, HG_CHUNK, HEAD_DIM)
    r_sub = lax.broadcasted_iota(jnp.int32, sub_shape, 1)
    r_chunk = lax.broadcasted_iota(jnp.int32, chunk_shape, 1)

    q, kk, g, v = [], [], [], []
    for sl in heads:
        lb = _lower_bound(lbl_ref[:, sl], layer)
        qh, kh, gh = _hgrn_gates(hq_ref[:, sl].astype(F32), hf_ref[:, sl], lb, jnp.log2)
        q.append(qh)
        kk.append(kh)
        g.append(gh)
        v.append(hi_ref[:, sl].astype(F32))
    v_bf = [x.astype(BF16) for x in v]

    b, c = [], []
    for n in every:
        g_hi, g_lo = _split_hi_lo(g[n])
        b.append(jnp.dot(tril_chunk, g_hi, preferred_element_type=F32)
                 + jnp.dot(tril_chunk, g_lo, preferred_element_type=F32))
        c.append(jnp.dot(tril_sub, g_hi, preferred_element_type=F32)
                 + jnp.dot(tril_sub, g_lo, preferred_element_type=F32))

    scores, shifted_v = [], []
    for n in every:
        q3, k3, c3, v3 = (a.reshape(sub_shape) for a in (q[n], kk[n], c[n], v[n]))
        prods, rolled = [q3 * k3], [v3]
        for delta in range(1, HG_SUB):
            ks = pltpu.roll(k3, delta, axis=1)
            cs = pltpu.roll(c3, delta, axis=1)
            prods.append(jnp.where(r_sub >= delta, q3 * ks * jnp.exp2(c3 - cs), 0.0))
            rolled.append(pltpu.roll(v3, delta, axis=1))
        prods = [x.reshape(tc, HEAD_DIM).astype(BF16) for x in prods]
        paired = [jnp.concatenate(prods[d:d + 2], axis=-1) for d in range(0, HG_SUB, 2)]
        scores.append(jnp.dot(jnp.concatenate(paired, axis=0), pair_ones,
                              preferred_element_type=F32))
        shifted_v.append(rolled)
    o = []
    for n in every:
        acc = jnp.zeros((tc, HEAD_DIM), F32)
        for delta in range(HG_SUB):
            rows = slice((delta // 2) * tc, (delta // 2 + 1) * tc)
            cols = slice((delta % 2) * HEAD_DIM, (delta % 2 + 1) * HEAD_DIM)
            acc = acc + scores[n][rows, cols] * shifted_v[n][delta].reshape(tc, HEAD_DIM)
        o.append(acc)

    for n in every:
        b3 = b[n].reshape(chunk_shape)
        kk3 = kk[n].reshape(chunk_shape)
        q_sub = (q[n] * jnp.exp2(c[n])).reshape(chunk_shape)
        q_cat, k_cat = [], []
        for i in range(1, n_sub):
            lo_row = i * HG_SUB
            b_start = b3[:, lo_row - 1:lo_row, :]
            k_cat.append(jnp.where(r_chunk < lo_row, kk3 * jnp.exp2(b_start - b3), 0.0).astype(BF16))
            q_cat.append(jnp.where((r_chunk >= lo_row) & (r_chunk < lo_row + HG_SUB),
                                   q_sub, 0.0).astype(BF16))
        a_off = jnp.einsum('ctk,csk->cts', jnp.concatenate(q_cat, axis=-1),
                           jnp.concatenate(k_cat, axis=-1), preferred_element_type=F32)
        o_off = jnp.einsum('cts,csv->ctv', a_off.astype(BF16), v_bf[n].reshape(chunk_shape),
                           preferred_element_type=F32)
        o[n] = o[n] + o_off.reshape(tc, HEAD_DIM)

    q_in = [(q[n] * jnp.exp2(b[n])).astype(BF16) for n in every]
    st = [st_ref[n] for n in every]
    o_inter = [[] for _ in every]
    for ci in range(n_chunks):
        rows = slice(ci * HG_CHUNK, (ci + 1) * HG_CHUNK)
        for n in every:
            b_c = b[n][rows]
            b_last = b_c[HG_CHUNK - 1:HG_CHUNK]
            o_inter[n].append(_dot_nt(q_in[n][rows], st[n].astype(BF16)))
            k_out = (kk[n][rows] * jnp.exp2(b_last - b_c)).astype(BF16)
            st[n] = st[n] * jnp.exp2(b_last) + _dot_tn(v_bf[n][rows], k_out)

    for n, sl in enumerate(heads):
        st_ref[n] = st[n]
        total = o[n] + jnp.concatenate(o_inter[n], axis=0)
        gate = _silu(hg_ref[:, sl].astype(F32))
        o_ref[:, sl] = (_rms_normalize(total, g_ref[:, sl]) * gate).astype(o_ref.dtype)

    @pl.when(t == pl.num_programs(2) - 1)
    def _():
        for n in every:
            s_ref[0, n] = st[n].T


def _hgrn_prompt(narrow, hf, lb_logits, g, batch, seq, tc, layer):
    nt = seq // tc
    n_layers = lb_logits.shape[0]
    width = HG_HEADS_PER_STEP * HEAD_DIM
    assert N_HEADS % HG_HEADS_PER_STEP == 0
    tok = lambda b, h, t: (b * nt + t, h)
    return pl.pallas_call(
        functools.partial(_hgrn_prompt_kernel, tc=tc, layer=layer),
        grid=(batch, N_HEADS // HG_HEADS_PER_STEP, nt),
        in_specs=[
            _narrow_spec(tc, width, 3, tok),
            pl.BlockSpec((tc, width), tok),
            _narrow_spec(tc, width, 5, tok),
            _narrow_spec(tc, width, 6, tok),
            pl.BlockSpec((n_layers, width), lambda b, h, t: (0, h)),
            pl.BlockSpec((1, width), lambda b, h, t: (0, h)),
        ],
        out_specs=[
            pl.BlockSpec((tc, width), tok),
            pl.BlockSpec((1, HG_HEADS_PER_STEP, HEAD_DIM, HEAD_DIM), lambda b, h, t: (b, h, 0, 0)),
        ],
        out_shape=[
            jax.ShapeDtypeStruct((batch * seq, GROUP_WIDTH), BF16),
            jax.ShapeDtypeStruct((batch, N_HEADS, HEAD_DIM, HEAD_DIM), F32),
        ],
        scratch_shapes=[pltpu.VMEM((HG_HEADS_PER_STEP, HEAD_DIM, HEAD_DIM), F32)],
        compiler_params=_params("parallel", "parallel", "arbitrary"),
        name="hgrn_prompt",
    )(narrow, hf, narrow, narrow, lb_logits, g)


def _hgrn_decode_kernel(hq_ref, hf_ref, hi_ref, hg_ref, s_ref, lbl_ref, g_ref, o_ref, so_ref,
                        *, t_new, layer, seqs_per_step):
    heads = [slice(h * HEAD_DIM, (h + 1) * HEAD_DIM) for h in range(N_HEADS)]
    pairs = [(t, u) for t in range(t_new) for u in range(t + 1)]
    pair_rows = -(-len(pairs) // 16) * 16
    ones = jnp.ones((HEAD_DIM, HEAD_DIM), BF16)
    lb = _lower_bound(lbl_ref[...], layer)
    for s in range(seqs_per_step):
        q, kk, g = _hgrn_gates(hq_ref[s].astype(F32), hf_ref[s], lb, jnp.log2)
        v = hi_ref[s].astype(F32)
        b_rows = [g[0:1]]
        for t in range(1, t_new):
            b_rows.append(b_rows[-1] + g[t:t + 1])
        b_last = b_rows[-1]
        b = jnp.concatenate(b_rows, axis=0)

        prods = [q[t:t + 1] * kk[u:u + 1] * jnp.exp2(b_rows[t] - b_rows[u]) for t, u in pairs]
        prods.append(jnp.zeros((pair_rows - len(pairs), GROUP_WIDTH), F32))
        prods = jnp.concatenate(prods, axis=0)
        a = jnp.dot(jnp.concatenate([prods[:, sl] for sl in heads], axis=0).astype(BF16), ones,
                    preferred_element_type=F32)

        q_in = (q * jnp.exp2(b)).astype(BF16)
        pad = jnp.zeros((HG_SUB - t_new, GROUP_WIDTH), F32)
        k_out = jnp.concatenate([kk * jnp.exp2(b_last - b), pad], axis=0).astype(BF16)
        v_pad = jnp.concatenate([v, pad], axis=0).astype(BF16)
        decay_t = jnp.concatenate([jnp.exp2(b_last[:, sl]) for sl in heads], axis=0).T
        gate = _silu(hg_ref[s].astype(F32))
        for h, sl in enumerate(heads):
            state = s_ref[0, s, h]
            o = jnp.dot(q_in[:, sl], state.astype(BF16), preferred_element_type=F32)
            rows = []
            for t in range(t_new):
                row = o[t:t + 1]
                for u in range(t + 1):
                    i = h * pair_rows + pairs.index((t, u))
                    row = row + a[i:i + 1] * v[u:u + 1, sl]
                rows.append(row)
            o = jnp.concatenate(rows, axis=0)
            o_ref[s, :, sl] = _rms_normalize(o, g_ref[:, sl]) * gate[:, sl]
            so_ref[0, s, h] = state * decay_t[:, h:h + 1] + _dot_tn(k_out[:, sl], v_pad[:, sl])


def _hgrn_decode(hq, hf, hi, hg, state, lb_logits, g, layer):
    n_seq, t_new, _ = hq.shape
    assert t_new <= HG_SUB
    n_layers = lb_logits.shape[0]
    seqs_per_step = next(n for n in (2, 1) if n_seq % n == 0)
    row_spec = pl.BlockSpec((seqs_per_step, t_new, GROUP_WIDTH), lambda s: (s, 0, 0))
    state_block = (1, seqs_per_step, N_HEADS, HEAD_DIM, HEAD_DIM)
    return pl.pallas_call(
        functools.partial(_hgrn_decode_kernel, t_new=t_new, layer=layer,
                          seqs_per_step=seqs_per_step),
        grid=(n_seq // seqs_per_step,),
        in_specs=[
            row_spec, row_spec, row_spec, row_spec,
            pl.BlockSpec(state_block, lambda s: (layer, s, 0, 0, 0)),
            pl.BlockSpec((n_layers, GROUP_WIDTH), lambda s: (0, 0)),
            pl.BlockSpec((1, GROUP_WIDTH), lambda s: (0, 0)),
        ],
        out_specs=[
            row_spec,
            pl.BlockSpec(state_block, lambda s: (0, s, 0, 0, 0)),
        ],
        out_shape=[
            jax.ShapeDtypeStruct((n_seq, t_new, GROUP_WIDTH), F32),
            jax.ShapeDtypeStruct((1, n_seq, N_HEADS, HEAD_DIM, HEAD_DIM), F32),
        ],
        compiler_params=_params("parallel"),
        name="hgrn_decode",
    )(hq, hf, hi, hg, state, lb_logits, g)


def _outproj_kernel(sb_ref, hg_ref, w_ref, x_ref, o_ref):
    half = sb_ref.shape[1]
    acc = jnp.dot(sb_ref[...].astype(BF16), w_ref[:half, :], preferred_element_type=F32)
    acc = acc + jnp.dot(hg_ref[...].astype(BF16), w_ref[half:, :], preferred_element_type=F32)
    o_ref[...] = x_ref[...] + acc


def _outproj(sb, hg, w, x, tm, tn):
    m, d = x.shape
    width = sb.shape[1]
    return pl.pallas_call(
        _outproj_kernel,
        grid=(m // tm, d // tn),
        in_specs=[
            pl.BlockSpec((tm, width), lambda i, j: (i, 0)),
            pl.BlockSpec((tm, width), lambda i, j: (i, 0)),
            pl.BlockSpec((2 * width, tn), lambda i, j: (0, j)),
            pl.BlockSpec((tm, tn), lambda i, j: (i, j)),
        ],
        out_specs=pl.BlockSpec((tm, tn), lambda i, j: (i, j)),
        out_shape=jax.ShapeDtypeStruct((m, d), F32),
        compiler_params=_params("parallel", "arbitrary"),
        name="outproj",
    )(sb, hg, w, x)


def _mlp_kernel(h_ref, g2_ref, wu_ref, wd_ref, gf_ref, y_ref, hn_ref, *, final_norm):
    j = pl.program_id(1)

    @pl.when(j == 0)
    def _():
        hn_ref[...] = _rms_normalize(h_ref[...], g2_ref[...]).astype(BF16)
        y_ref[...] = jnp.zeros_like(y_ref)

    u = jnp.dot(hn_ref[...], wu_ref[...], preferred_element_type=F32)
    a = jnp.square(jnp.maximum(u, 0.0)).astype(BF16)
    y_ref[...] += jnp.dot(a, wd_ref[...], preferred_element_type=F32)

    @pl.when(j == pl.num_programs(1) - 1)
    def _():
        y = h_ref[...] + y_ref[...]
        y_ref[...] = _rms_normalize(y, gf_ref[...]) if final_norm else y


def _mlp(h, g2, wu_bf16, wd_bf16, gf, tm, tf, final_norm):
    m, d = h.shape
    f = wu_bf16.shape[1]
    return pl.pallas_call(
        functools.partial(_mlp_kernel, final_norm=final_norm),
        grid=(m // tm, f // tf),
        in_specs=[
            pl.BlockSpec((tm, d), lambda i, j: (i, 0)),
            pl.BlockSpec((1, d), lambda i, j: (0, 0)),
            pl.BlockSpec((d, tf), lambda i, j: (0, j)),
            pl.BlockSpec((tf, d), lambda i, j: (j, 0)),
            pl.BlockSpec((1, d), lambda i, j: (0, 0)),
        ],
        out_specs=pl.BlockSpec((tm, d), lambda i, j: (i, 0)),
        out_shape=jax.ShapeDtypeStruct((m, d), F32),
        scratch_shapes=[pltpu.VMEM((tm, d), BF16)],
        compiler_params=_params("parallel", "arbitrary"),
        name="mlp",
    )(h, g2, wu_bf16, wd_bf16, gf)


def _mlp_decode_kernel(pt_ref, h_ref, g2_ref, wu_ref, wd_ref, gf_ref, *refs, final_norm,
                       page_steps, **decode_static):
    del pt_ref
    decode_refs, y_ref, sb_ref = refs[:-5], refs[-5], refs[-4]
    hn_ref, dec_acc_ref, dec_carry_ref = refs[-3:]
    j = pl.program_id(1)
    step = pl.program_id(0) * pl.num_programs(1) + j
    logits, sums, weights = _sb_decode_stages(step % page_steps, page_steps, *decode_refs, sb_ref,
                                              dec_acc_ref, dec_carry_ref, **decode_static)

    @pl.when(j == 0)
    def _():
        hn_ref[...] = _rms_normalize(h_ref[...], g2_ref[...]).astype(BF16)
        y_ref[...] = jnp.zeros_like(y_ref)

    logits()
    u = jnp.dot(hn_ref[...], wu_ref[...], preferred_element_type=F32)
    sums()
    a = jnp.square(jnp.maximum(u, 0.0)).astype(BF16)
    y_ref[...] += jnp.dot(a, wd_ref[...], preferred_element_type=F32)
    weights()

    @pl.when(j == pl.num_programs(1) - 1)
    def _():
        y = h_ref[...] + y_ref[...]
        y_ref[...] = _rms_normalize(y, gf_ref[...]) if final_norm else y


def _mlp_decode_fits(m, f, tm, tf, n_seq, n_pages):
    pages_per_step = next(n for n in (4, 2, 1) if n_pages % n == 0)
    seqs_per_step = next(n for n in (2, 1) if n_seq % n == 0)
    return (m // tm) * (f // tf) == (n_seq // seqs_per_step) * (n_pages // pages_per_step)


def _mlp_decode(h, g2, wu_bf16, wd_bf16, gf, tm, tf, final_norm, *decode_args):
    m, d = h.shape
    f = wu_bf16.shape[1]
    nj = f // tf
    page_table = decode_args[6]
    n_pages = page_table.shape[1]

    def locate(i, j):
        step = i * nj + j
        return step // page_steps, step % page_steps

    plan = _sb_decode_plan(*decode_args, locate=locate)
    page_steps = plan["grid"][1]
    assert (m // tm) * nj == plan["grid"][0] * page_steps
    return pl.pallas_call(
        functools.partial(_mlp_decode_kernel, final_norm=final_norm, page_steps=page_steps,
                          **plan["static"]),
        grid_spec=pltpu.PrefetchScalarGridSpec(
            num_scalar_prefetch=1,
            grid=(m // tm, nj),
            in_specs=[
                pl.BlockSpec((tm, d), lambda i, j, pt: (i, 0)),
                pl.BlockSpec((1, d), lambda i, j, pt: (0, 0)),
                pl.BlockSpec((d, tf), lambda i, j, pt: (0, j)),
                pl.BlockSpec((tf, d), lambda i, j, pt: (j, 0)),
                pl.BlockSpec((1, d), lambda i, j, pt: (0, 0)),
            ] + plan["in_specs"],
            out_specs=[pl.BlockSpec((tm, d), lambda i, j, pt: (i, 0)), plan["out_spec"]],
            scratch_shapes=[pltpu.VMEM((tm, d), BF16)] + plan["scratch"],
        ),
        out_shape=[jax.ShapeDtypeStruct((m, d), F32), plan["out_shape"]],
        compiler_params=pltpu.CompilerParams(dimension_semantics=("arbitrary", "arbitrary"),
                                             vmem_limit_bytes=VMEM_LIMIT_FUSED),
        name="mlp_decode",
    )(page_table.reshape(-1), h, g2, wu_bf16, wd_bf16, gf, *plan["operands"])


def _row_tile(m, target):
    return target if m % target == 0 else m


def kernel(x_prompt, x_sample, cache_k, cache_v, state_hgrn, page_table, norm1_g, w_in, sb_bias,
           sb_norm_g, hg_norm_g, hg_lb_logits, w_out, norm2_g, w_up, w_down, final_norm_g):
    batch, seq, d = x_prompt.shape
    n_seq, t_new, _ = x_sample.shape
    depth = w_in.shape[0]
    hp = x_prompt.reshape(batch * seq, d)
    hs = x_sample.reshape(n_seq * t_new, d)
    tm_in_p = _row_tile(batch * seq, 1024)
    tm_out_p = _row_tile(batch * seq, 1024)
    tm_p = _row_tile(batch * seq, 512)
    tm_s = _row_tile(n_seq * t_new, 512)
    tn_in = 512
    tf = 1024
    tq =_row_tile(seq, 512)
    tk_sb = _row_tile(tq, 256)
    tc = _row_tile(seq, 256)
    gf = final_norm_g.reshape(1, d)

    outs = {k: [] for k in ("kp", "vp", "sp", "ks", "vs", "ss")}
    for l in range(depth):
        g1 = norm1_g[l].reshape(1, d)
        g2 = norm2_g[l].reshape(1, d)
        sbg = sb_norm_g[l].reshape(1, GROUP_WIDTH)
        hgg = hg_norm_g[l].reshape(1, GROUP_WIDTH)
        w_in_l = w_in[l].astype(BF16)
        w_out_l = w_out[l].astype(BF16)
        w_up_l = w_up[l].astype(BF16)
        w_down_l = w_down[l].astype(BF16)
        last = l == depth - 1

        sk, sv, hf, narrow = _inproj(hp, g1, w_in_l, tm_in_p, tn_in)
        sb = _sb_prompt(narrow, sk, sv, sb_bias[l], sbg, batch, seq, tq, tk_sb)
        hgo, s_p = _hgrn_prompt(narrow, hf, hg_lb_logits, hgg, batch, seq, tc, l)
        hmid = _outproj(sb, hgo, w_out_l, hp, tm_out_p, 1024)
        outs["kp"].append(sk.reshape(batch, seq, N_HEADS, HEAD_DIM))
        outs["vp"].append(sv.reshape(batch, seq, N_HEADS, HEAD_DIM))
        outs["sp"].append(s_p)

        *wide_s, narrow_s = _inproj(hs, g1, w_in_l, tm_s, tn_in)
        tk, tv, uf = (a.reshape(n_seq, t_new, GROUP_WIDTH) for a in wide_s)
        tq_, uq, ui, ug = (narrow_s[n].reshape(n_seq, t_new, GROUP_WIDTH) for n in range(4))
        rows = N_HEADS * t_new
        sbg_rows = jnp.repeat(sbg.reshape(N_HEADS, HEAD_DIM), t_new, axis=0)
        bias_rows = jnp.broadcast_to(jnp.repeat(sb_bias[l], t_new)[:, None], (rows, HEAD_DIM))
        q_rows = tq_.reshape(n_seq, t_new, N_HEADS, HEAD_DIM).transpose(0, 2, 1, 3)
        decode_args = (q_rows.reshape(n_seq, rows, HEAD_DIM), bias_rows,
                       tk.reshape(n_seq, rows, HEAD_DIM), tv.reshape(n_seq, rows, HEAD_DIM),
                       cache_k, cache_v, page_table, sbg_rows, l, t_new)
        if _mlp_decode_fits(batch * seq, w_up_l.shape[1], tm_p, tf, n_seq, page_table.shape[1]):
            hp, sb2 = _mlp_decode(hmid, g2, w_up_l, w_down_l, gf, tm_p, tf, last, *decode_args)
        else:
            hp = _mlp(hmid, g2, w_up_l, w_down_l, gf, tm_p, tf, last)
            sb2 = _sb_decode(*decode_args)
        sb2 = sb2.reshape(n_seq, N_HEADS, t_new, HEAD_DIM).transpose(0, 2, 1, 3)
        sb2 = sb2.reshape(n_seq * t_new, GROUP_WIDTH)
        hgo2, s_s = _hgrn_decode(uq, uf, ui, ug, state_hgrn, hg_lb_logits, hgg, l)
        hmid2 = _outproj(sb2, hgo2.reshape(n_seq * t_new, GROUP_WIDTH), w_out_l, hs, tm_s, 1024)
        hs = _mlp(hmid2, g2, w_up_l, w_down_l, gf, tm_s, tf, last)
        outs["ks"].append(tk.reshape(n_seq, t_new, N_HEADS, HEAD_DIM))
        outs["vs"].append(tv.reshape(n_seq, t_new, N_HEADS, HEAD_DIM))
        outs["ss"].append(s_s[0])

    y_prompt = hp.reshape(batch, seq, d)
    y_sample = hs.reshape(n_seq, t_new, d)
    return (y_prompt, y_sample, jnp.stack(outs["kp"]), jnp.stack(outs["vp"]), jnp.stack(outs["sp"]),
            jnp.stack(outs["ks"]), jnp.stack(outs["vs"]), jnp.stack(outs["ss"]))
```
